```python
import jax, jax.numpy as jnp
from jax import lax
import numpy as np

D_MODEL = 2048
BATCH = 16
SEQ = 2048
DEPTH = 4

CHUNK = 64
LEFT_CHUNKS = 8
BAND = (LEFT_CHUNKS + 1) * CHUNK
N_HEADS_A = 16
HEAD_DIM = 64
D_ATTN = N_HEADS_A * HEAD_DIM
MAX_REL = 256
N_REL = 2 * MAX_REL + 1
D_CONV = D_MODEL // 2
CONV_WIDTH = 31
D_FF = 7168
N_EXPERTS = 8
TOP_K = 2
N_DENSE = (DEPTH + 1) // 2
N_MOE = DEPTH // 2
EPS = 1e-6
Q_OFF = 0
K_OFF = D_ATTN
V_OFF = 2 * D_ATTN
CONV_OFF = 3 * D_ATTN
GATE_OFF = CONV_OFF + 2 * D_CONV
D_IN = GATE_OFF + 2 * D_MODEL

kernel_name = "hybrid_chunk_attn_conformer_conv_moe"


def rms_norm(x, g):
    xf = x.astype(jnp.float32)
    y = xf * lax.rsqrt(jnp.mean(xf * xf, axis=-1, keepdims=True) + EPS)
    return (y * g.astype(jnp.float32)).astype(x.dtype)


def layer_norm(x, g, b):
    xf = x.astype(jnp.float32)
    mu = jnp.mean(xf, axis=-1, keepdims=True)
    xc = xf - mu
    var = jnp.mean(xc * xc, axis=-1, keepdims=True)
    y = xc * lax.rsqrt(var + EPS) * g.astype(jnp.float32) + b.astype(jnp.float32)
    return y.astype(x.dtype)


def chunked_band_attention(q, k, v, rel_bias):
    B, S, H, Dh = q.shape
    n_chunks = S // CHUNK
    pad = LEFT_CHUNKS * CHUNK
    k_pad = jnp.pad(k, ((0, 0), (pad, 0), (0, 0), (0, 0)))
    v_pad = jnp.pad(v, ((0, 0), (pad, 0), (0, 0), (0, 0)))
    q_c = jnp.moveaxis(q.reshape(B, n_chunks, CHUNK, H, Dh), 1, 0)
    rel = (pad + jnp.arange(CHUNK))[:, None] - jnp.arange(BAND)[None, :]
    bias = rel_bias.astype(jnp.float32)[:, jnp.clip(rel, -MAX_REL, MAX_REL) + MAX_REL]
    scale = Dh ** -0.5

    def one_chunk(args):
        c, qc = args
        start = c * CHUNK
        kb = lax.dynamic_slice_in_dim(k_pad, start, BAND, axis=1)
        vb = lax.dynamic_slice_in_dim(v_pad, start, BAND, axis=1)
        s = jnp.einsum('bqhd,bkhd->bhqk', qc, kb, preferred_element_type=jnp.float32) * scale + bias
        valid = (start + jnp.arange(BAND)) >= pad
        s = jnp.where(valid[None, None, None, :], s, -1e30)
        p = jax.nn.softmax(s, axis=-1).astype(vb.dtype)
        return jnp.einsum('bhqk,bkhd->bqhd', p, vb)

    out = lax.map(one_chunk, (jnp.arange(n_chunks), q_c))
    return jnp.moveaxis(out, 0, 1).reshape(B, S, H * Dh)


def conformer_conv(u, w_dw, b_dw, ln_g, ln_b):
    a, g = jnp.split(u, 2, axis=-1)
    y = a * jax.nn.sigmoid(g)
    y = lax.conv_general_dilated(
        y, w_dw[:, None, :], window_strides=(1,),
        padding=[(CONV_WIDTH - 1, 0)],
        dimension_numbers=('NWC', 'WIO', 'NWC'),
        feature_group_count=D_CONV) + b_dw
    y = layer_norm(y, ln_g, ln_b)
    return jax.nn.silu(y)


def swiglu(h, w_gate, w_up, w_down):
    return (jax.nn.silu(h @ w_gate) * (h @ w_up)) @ w_down


def moe_swiglu(h, w_router, w_gate, w_up, w_down):
    logits = (h @ w_router).astype(jnp.float32)
    top_v, top_i = lax.top_k(logits, TOP_K)
    top_w = jax.nn.softmax(top_v, axis=-1)
    gates = jnp.sum(jax.nn.one_hot(top_i, N_EXPERTS, dtype=jnp.float32) * top_w[..., None], axis=-2)
    gates = gates.astype(h.dtype)
    out = gates[..., 0:1] * swiglu(h, w_gate[0], w_up[0], w_down[0])
    for e in range(1, N_EXPERTS):
        out = out + gates[..., e:e + 1] * swiglu(h, w_gate[e], w_up[e], w_down[e])
    return out


def setup_inputs(seed: int = 0) -> dict:
    key = jax.random.key(seed)
    ks = jax.random.split(key, 24)
    f32 = jnp.float32

    def nrm(k, shape, fan_in):
        return jax.random.normal(k, shape, f32) * (fan_in ** -0.5)

    def gain(k, shape):
        return 1.0 + 0.02 * jax.random.normal(k, shape, f32)

    return {
        "x": jax.random.normal(ks[0], (BATCH, SEQ, D_MODEL), f32),
        "mix_norm_g": gain(ks[1], (DEPTH, D_MODEL)),
        "w_in": nrm(ks[2], (DEPTH, D_MODEL, D_IN), D_MODEL),
        "q_norm_g": gain(ks[3], (DEPTH, HEAD_DIM)),
        "k_norm_g": gain(ks[4], (DEPTH, HEAD_DIM)),
        "rel_bias": 0.5 * jax.random.normal(ks[5], (N_HEADS_A, N_REL), f32),
        "conv_dw_w": nrm(ks[6], (DEPTH, CONV_WIDTH, D_CONV), CONV_WIDTH),
        "conv_dw_b": 0.02 * jax.random.normal(ks[7], (DEPTH, D_CONV), f32),
        "conv_ln_g": gain(ks[8], (DEPTH, D_CONV)),
        "conv_ln_b": 0.02 * jax.random.normal(ks[9], (DEPTH, D_CONV), f32),
        "w_branch_attn": nrm(ks[10], (DEPTH, D_ATTN, D_MODEL), D_ATTN),
        "w_branch_conv": nrm(ks[11], (DEPTH, D_CONV, D_MODEL), D_CONV),
        "w_out": nrm(ks[12], (DEPTH, D_MODEL, D_MODEL), D_MODEL),
        "ffn_norm_g": gain(ks[13], (DEPTH, D_MODEL)),
        "dense_w_gate": nrm(ks[14], (N_DENSE, D_MODEL, D_FF), D_MODEL),
        "dense_w_up": nrm(ks[15], (N_DENSE, D_MODEL, D_FF), D_MODEL),
        "dense_w_down": nrm(ks[16], (N_DENSE, D_FF, D_MODEL), D_FF),
        "moe_router": nrm(ks[17], (N_MOE, D_MODEL, N_EXPERTS), D_MODEL),
        "moe_w_gate": nrm(ks[18], (N_MOE, N_EXPERTS, D_MODEL, D_FF), D_MODEL),
        "moe_w_up": nrm(ks[19], (N_MOE, N_EXPERTS, D_MODEL, D_FF), D_MODEL),
        "moe_w_down": nrm(ks[20], (N_MOE, N_EXPERTS, D_FF, D_MODEL), D_FF),
    }


def reference(x, mix_norm_g, w_in, q_norm_g, k_norm_g, rel_bias, conv_dw_w, conv_dw_b,
              conv_ln_g, conv_ln_b, w_branch_attn, w_branch_conv, w_out, ffn_norm_g,
              dense_w_gate, dense_w_up, dense_w_down, moe_router, moe_w_gate, moe_w_up,
              moe_w_down):
    B, S, _ = x.shape
    for l in range(DEPTH):
        h = rms_norm(x, mix_norm_g[l])
        p = h @ w_in[l]
        q = rms_norm(p[..., Q_OFF:K_OFF].reshape(B, S, N_HEADS_A, HEAD_DIM), q_norm_g[l])
        k = rms_norm(p[..., K_OFF:V_OFF].reshape(B, S, N_HEADS_A, HEAD_DIM), k_norm_g[l])
        v = p[..., V_OFF:CONV_OFF].reshape(B, S, N_HEADS_A, HEAD_DIM)
        a = chunked_band_attention(q, k, v, rel_bias)
        c = conformer_conv(p[..., CONV_OFF:GATE_OFF], conv_dw_w[l], conv_dw_b[l],
                           conv_ln_g[l], conv_ln_b[l])
        gate_a = jax.nn.sigmoid(p[..., GATE_OFF:GATE_OFF + D_MODEL])
        gate_c = jax.nn.sigmoid(p[..., GATE_OFF + D_MODEL:])
        merged = gate_a * (a @ w_branch_attn[l]) + gate_c * (c @ w_branch_conv[l])
        x = x + merged @ w_out[l]
        h = rms_norm(x, ffn_norm_g[l])
        if l % 2 == 0:
            i = l // 2
            f = swiglu(h, dense_w_gate[i], dense_w_up[i], dense_w_down[i])
        else:
            i = l // 2
            f = moe_swiglu(h, moe_router[i], moe_w_gate[i], moe_w_up[i], moe_w_down[i])
        x = x + f
    return x
```

```python
import functools

import jax
import jax.numpy as jnp
from jax import lax
from jax.experimental import pallas as pl
from jax.experimental.pallas import tpu as pltpu

F32 = jnp.float32
BF16 = jnp.bfloat16

D_MODEL = 2048
CHUNK = 64
LEFT = 8 * CHUNK
N_HEADS = 16
HEAD_DIM = 64
D_ATTN = N_HEADS * HEAD_DIM
MAX_REL = 256
D_CONV = 1024
CONV_WIDTH = 31
D_FF = 7168
N_EXPERTS = 8
EPS = 1e-6
D_IN = 9216
COL_BLOCK = 1024

LANES = 128
V7X_VMEM_BYTES = 64 * 1024 * 1024
VMEM_LIMIT = V7X_VMEM_BYTES - 8 * 1024 * 1024

NEG_INF = -1e30


def _params(*sem):
    return pltpu.CompilerParams(dimension_semantics=sem, vmem_limit_bytes=VMEM_LIMIT)


def _rms_rows_to(dst_ref, x_ref, g_ref, rows=64):
    def body(r, carry):
        sl = pl.ds(pl.multiple_of(r * rows, rows), rows)
        x = x_ref[sl, :]
        ms = jnp.mean(x * x, axis=-1, keepdims=True)
        dst_ref[sl, :] = (x * lax.rsqrt(ms + EPS) * g_ref[...]).astype(dst_ref.dtype)
        return carry

    lax.fori_loop(0, x_ref.shape[0] // rows, body, 0)


def _in_proj_kernel(x_ref, g_ref, w_ref, o_ref, h_ref):
    @pl.when(pl.program_id(1) == 0)
    def _():
        _rms_rows_to(h_ref, x_ref, g_ref)

    o_ref[...] = jnp.dot(h_ref[...], w_ref[...], preferred_element_type=F32).astype(o_ref.dtype)


def _in_proj(x, g, w):
    T = x.shape[0]
    tm = min(1024, T)
    tn = COL_BLOCK
    return pl.pallas_call(
        _in_proj_kernel,
        out_shape=jax.ShapeDtypeStruct((T, D_IN), BF16),
        grid=(T // tm, D_IN // tn),
        in_specs=[
            pl.BlockSpec((tm, D_MODEL), lambda i, j: (i, 0)),
            pl.BlockSpec((1, D_MODEL), lambda i, j: (0, 0)),
            pl.BlockSpec((D_MODEL, tn), lambda i, j: (0, j)),
        ],
        out_specs=pl.BlockSpec((tm, tn), lambda i, j: (i, j)),
        scratch_shapes=[pltpu.VMEM((tm, D_MODEL), BF16)],
        compiler_params=_params("parallel", "arbitrary"),
        name="in_proj",
    )(x, g, w)


def _pair_rms_scale(xp):
    lane = lax.broadcasted_iota(jnp.int32, xp.shape, 1)
    lo = lane < HEAD_DIM
    sq = xp * xp
    s_lo = jnp.sum(jnp.where(lo, sq, 0.0), axis=-1, keepdims=True)
    s_hi = jnp.sum(jnp.where(lo, 0.0, sq), axis=-1, keepdims=True)
    ms = jnp.where(lo, s_lo, s_hi) * (1.0 / HEAD_DIM)
    return lax.rsqrt(ms + EPS)


def _attn_kernel(q_ref, k_ref, v_ref, bias_ref, gq_ref, gk_ref, o_ref, kn_ref, qn_ref, *, qrows, win):
    qb = pl.program_id(1)
    n_pairs = D_ATTN // LANES

    @pl.when(qb == 0)
    def _():
        rows = 128

        def body(r, carry):
            sl = pl.ds(pl.multiple_of(r * rows, rows), rows)
            for hp in range(n_pairs):
                cs = slice(hp * LANES, (hp + 1) * LANES)
                kp = k_ref[sl, cs].astype(F32)
                kn_ref[sl, cs] = (kp * _pair_rms_scale(kp) * gk_ref[...]).astype(BF16)
            return carry

        lax.fori_loop(0, k_ref.shape[0] // rows, body, 0)

    for hp in range(n_pairs):
        cs = slice(hp * LANES, (hp + 1) * LANES)
        qp = q_ref[:, cs].astype(F32)
        qn_ref[:, cs] = (qp * _pair_rms_scale(qp) * gq_ref[...]).astype(BF16)

    ws = pl.multiple_of(jnp.maximum(qb * qrows - LEFT, 0), CHUNK)
    lane = lax.broadcasted_iota(jnp.int32, (qrows, LANES), 1)
    lo = lane < HEAD_DIM
    for hp in range(n_pairs):
        cs = slice(hp * LANES, (hp + 1) * LANES)
        kw = kn_ref[pl.ds(ws, win), cs]
        vw = v_ref[pl.ds(ws, win), cs]
        qp = qn_ref[:, cs]
        outs = []
        for j in range(2):
            qm = jnp.where(lo if j == 0 else jnp.logical_not(lo), qp, jnp.zeros_like(qp))
            s = lax.dot_general(qm, kw, (((1,), (1,)), ((), ())), preferred_element_type=F32)
            s = s + bias_ref[2 * hp + j]
            m = jnp.max(s, axis=-1, keepdims=True)
            e = jnp.exp(s - m)
            l = jnp.sum(e, axis=-1, keepdims=True)
            o = jnp.dot(e.astype(BF16), vw, preferred_element_type=F32)
            outs.append(o / l)
        o_ref[:, cs] = jnp.where(lo, outs[0], outs[1]).astype(o_ref.dtype)


def _bias_table(rel_bias, qrows):
    win = LEFT + qrows
    n_var = LEFT // qrows + 1
    off = (jnp.arange(n_var) * qrows)[:, None, None]
    qi = off + jnp.arange(qrows)[None, :, None]
    kj = jnp.arange(win)[None, None, :]
    rel = jnp.clip(qi - kj, -MAX_REL, MAX_REL) + MAX_REL
    dchunk = qi // CHUNK - kj // CHUNK
    valid = (dchunk >= 0) & (dchunk <= LEFT // CHUNK)
    bias = rel_bias.astype(F32)[:, rel]
    bias = jnp.where(valid[None], bias, NEG_INF)
    return jnp.transpose(bias, (1, 0, 2, 3))


def _attention(p, bias_tab, gq, gk, B, S, qrows):
    T = B * S
    win = LEFT + qrows
    nq = S // qrows
    n_var = bias_tab.shape[0]
    scale = HEAD_DIM ** -0.5
    gq2 = (jnp.tile(gq.astype(F32), 2) * scale)[None]
    gk2 = jnp.tile(gk.astype(F32), 2)[None]
    kern = functools.partial(_attn_kernel, qrows=qrows, win=win)
    return pl.pallas_call(
        kern,
        out_shape=jax.ShapeDtypeStruct((T, D_ATTN), BF16),
        grid=(B, nq),
        in_specs=[
            pl.BlockSpec((qrows, COL_BLOCK), lambda b, q: (b * nq + q, 0)),
            pl.BlockSpec((S, COL_BLOCK), lambda b, q: (b, 1), pipeline_mode=pl.Buffered(1)),
            pl.BlockSpec((S, COL_BLOCK), lambda b, q: (b, 2), pipeline_mode=pl.Buffered(1)),
            pl.BlockSpec((None, N_HEADS, qrows, win), lambda b, q: (jnp.minimum(q, n_var - 1), 0, 0, 0)),
            pl.BlockSpec((1, LANES), lambda b, q: (0, 0)),
            pl.BlockSpec((1, LANES), lambda b, q: (0, 0)),
        ],
        out_specs=pl.BlockSpec((qrows, D_ATTN), lambda b, q: (b * nq + q, 0)),
        scratch_shapes=[pltpu.VMEM((S, D_ATTN), BF16), pltpu.VMEM((qrows, D_ATTN), BF16)],
        compiler_params=_params("parallel", "arbitrary"),
        name="band_attention",
    )(p, p, p, bias_tab, gq2, gk2)


CONV_HIST = 32
CONV_ROWS = 16


def _conv_kernel(a_ref, g_ref, w_ref, b_ref, lg_ref, lb_ref, o_ref, y_ref, *, tt):
    t = pl.program_id(1)

    @pl.when(t == 0)
    def _():
        y_ref[0:CONV_HIST, :] = jnp.zeros((CONV_HIST, D_CONV), F32)

    @pl.when(t > 0)
    def _():
        y_ref[0:CONV_HIST, :] = y_ref[tt:tt + CONV_HIST, :]

    a = a_ref[...].astype(F32)
    g = g_ref[...].astype(F32)
    y_ref[CONV_HIST:CONV_HIST + tt, :] = a * jax.nn.sigmoid(g)

    shift = CONV_HIST - (CONV_WIDTH - 1)
    for rb in range(tt // CONV_ROWS):
        r0 = rb * CONV_ROWS
        acc = jnp.broadcast_to(b_ref[...], (CONV_ROWS, D_CONV))
        for j in range(CONV_WIDTH):
            acc = acc + w_ref[j:j + 1, :] * y_ref[r0 + shift + j:r0 + shift + j + CONV_ROWS, :]
        mu = jnp.mean(acc, axis=-1, keepdims=True)
        xc = acc - mu
        var = jnp.mean(xc * xc, axis=-1, keepdims=True)
        z = xc * lax.rsqrt(var + EPS) * lg_ref[...] + lb_ref[...]
        o_ref[r0:r0 + CONV_ROWS, :] = (z * jax.nn.sigmoid(z)).astype(o_ref.dtype)


def _conv_branch(p, w_dw, b_dw, ln_g, ln_b, B, S, tt):
    T = B * S
    nt = S // tt
    w_pad = jnp.zeros((CONV_HIST, D_CONV), F32).at[:CONV_WIDTH].set(w_dw.astype(F32))
    row = lambda v: v.astype(F32)[None]
    vec = pl.BlockSpec((1, D_CONV), lambda b, t: (0, 0))
    return pl.pallas_call(
        functools.partial(_conv_kernel, tt=tt),
        out_shape=jax.ShapeDtypeStruct((T, D_CONV), BF16),
        grid=(B, nt),
        in_specs=[
            pl.BlockSpec((tt, COL_BLOCK), lambda b, t: (b * nt + t, 3)),
            pl.BlockSpec((tt, COL_BLOCK), lambda b, t: (b * nt + t, 4)),
            pl.BlockSpec((CONV_HIST, D_CONV), lambda b, t: (0, 0)),
            vec, vec, vec,
        ],
        out_specs=pl.BlockSpec((tt, D_CONV), lambda b, t: (b * nt + t, 0)),
        scratch_shapes=[pltpu.VMEM((CONV_HIST + tt, D_CONV), F32)],
        compiler_params=_params("parallel", "arbitrary"),
        name="conv_branch",
    )(p, p, w_pad, row(b_dw), row(ln_g), row(ln_b))


MERGE_COLS = 512


def _merge_kernel(a_ref, c_ref, ga0_ref, ga1_ref, gc0_ref, gc1_ref, x_ref, wa_ref, wb_ref, wo_ref, o_ref, m_ref):
    ga_refs = (ga0_ref, ga1_ref)
    gc_refs = (gc0_ref, gc1_ref)
    per_block = COL_BLOCK // MERGE_COLS
    for n in range(D_MODEL // MERGE_COLS):
        cs = slice(n * MERGE_COLS, (n + 1) * MERGE_COLS)
        gs = slice((n % per_block) * MERGE_COLS, (n % per_block + 1) * MERGE_COLS)
        ya = jnp.dot(a_ref[...], wa_ref[:, cs], preferred_element_type=F32)
        yc = jnp.dot(c_ref[...], wb_ref[:, cs], preferred_element_type=F32)
        ga = jax.nn.sigmoid(ga_refs[n // per_block][:, gs].astype(F32))
        gc = jax.nn.sigmoid(gc_refs[n // per_block][:, gs].astype(F32))
        m_ref[:, cs] = (ga * ya + gc * yc).astype(BF16)
    for n in range(D_MODEL // MERGE_COLS):
        cs = slice(n * MERGE_COLS, (n + 1) * MERGE_COLS)
        o_ref[:, cs] = x_ref[:, cs] + jnp.dot(m_ref[...], wo_ref[:, cs], preferred_element_type=F32)


def _merge(p, a, c, x, wa, wb, wo):
    T = x.shape[0]
    tm = min(512, T)
    gate = lambda blk: pl.BlockSpec((tm, COL_BLOCK), lambda i: (i, blk))
    resident = lambda shape: pl.BlockSpec(shape, lambda i: (0, 0), pipeline_mode=pl.Buffered(1))
    return pl.pallas_call(
        _merge_kernel,
        out_shape=jax.ShapeDtypeStruct((T, D_MODEL), F32),
        grid=(T // tm,),
        in_specs=[
            pl.BlockSpec((tm, D_ATTN), lambda i: (i, 0)),
            pl.BlockSpec((tm, D_CONV), lambda i: (i, 0)),
            gate(5), gate(6), gate(7), gate(8),
            pl.BlockSpec((tm, D_MODEL), lambda i: (i, 0)),
            resident((D_ATTN, D_MODEL)),
            resident((D_CONV, D_MODEL)),
            resident((D_MODEL, D_MODEL)),
        ],
        out_specs=pl.BlockSpec((tm, D_MODEL), lambda i: (i, 0)),
        scratch_shapes=[pltpu.VMEM((tm, D_MODEL), BF16)],
        compiler_params=_params("parallel"),
        name="merge_out_proj",
    )(a, c, p, p, p, p, x, wa, wb, wo)


def _ffn_kernel(te_ref, ta_ref, x_ref, g_ref, wg_ref, wu_ref, wd_ref, o_ref, h_ref, *, residual):
    del te_ref
    i = pl.program_id(0)
    f = pl.program_id(1)

    @pl.when(f == 0)
    def _():
        _rms_rows_to(h_ref, x_ref, g_ref)
        if residual:
            o_ref[...] = x_ref[...]
        else:
            o_ref[...] = jnp.zeros(o_ref.shape, o_ref.dtype)

    @pl.when(ta_ref[i] == 1)
    def _():
        h = h_ref[...]
        gt = jnp.dot(h, wg_ref[...], preferred_element_type=F32)
        up = jnp.dot(h, wu_ref[...], preferred_element_type=F32)
        act = (gt * jax.nn.sigmoid(gt) * up).astype(BF16)
        o_ref[...] += jnp.dot(act, wd_ref[...], preferred_element_type=F32)


def _ffn(x, g, wg, wu, wd, tile_expert, tile_active, *, residual, tm):
    R = x.shape[0]
    tf = 512
    nf = D_FF // tf
    fsel = lambda i, f, te, ta: jnp.where(ta[i] == 1, f, nf - 1)
    grid_spec = pltpu.PrefetchScalarGridSpec(
        num_scalar_prefetch=2,
        grid=(R // tm, nf),
        in_specs=[
            pl.BlockSpec((tm, D_MODEL), lambda i, f, te, ta: (i, 0)),
            pl.BlockSpec((1, D_MODEL), lambda i, f, te, ta: (0, 0)),
            pl.BlockSpec((None, D_MODEL, tf), lambda i, f, te, ta: (te[i], 0, fsel(i, f, te, ta))),
            pl.BlockSpec((None, D_MODEL, tf), lambda i, f, te, ta: (te[i], 0, fsel(i, f, te, ta))),
            pl.BlockSpec((None, tf, D_MODEL), lambda i, f, te, ta: (te[i], fsel(i, f, te, ta), 0)),
        ],
        out_specs=pl.BlockSpec((tm, D_MODEL), lambda i, f, te, ta: (i, 0)),
        scratch_shapes=[pltpu.VMEM((tm, D_MODEL), BF16)],
    )
    return pl.pallas_call(
        functools.partial(_ffn_kernel, residual=residual),
        out_shape=jax.ShapeDtypeStruct((R, D_MODEL), F32),
        grid_spec=grid_spec,
        compiler_params=_params("parallel", "arbitrary"),
        name="swiglu_residual" if residual else "swiglu_grouped",
    )(tile_expert, tile_active, x, g, wg, wu, wd)


ROUTER_ROWS = 128


def _router_kernel(x_ref, g_ref, wh_ref, wl_ref, meta_ref, gate_ref, cnt_ref, tri_ref, carry_ref, logit_ref):
    i = pl.program_id(0)
    tm = x_ref.shape[0]

    @pl.when(i == 0)
    def _():
        r = lax.broadcasted_iota(jnp.int32, (tm, tm), 0)
        c = lax.broadcasted_iota(jnp.int32, (tm, tm), 1)
        tri_ref[...] = jnp.where(c < r, 1.0, 0.0).astype(BF16)
        carry_ref[...] = jnp.zeros(carry_ref.shape, F32)

    def body(r, carry):
        sl = pl.ds(pl.multiple_of(r * ROUTER_ROWS, ROUTER_ROWS), ROUTER_ROWS)
        x = x_ref[sl, :]
        ms = jnp.mean(x * x, axis=-1, keepdims=True)
        h = x * lax.rsqrt(ms + EPS) * g_ref[...]
        hi = h.astype(BF16)
        lo = (h - hi.astype(F32)).astype(BF16)
        logit_ref[sl, :] = (jnp.dot(hi, wh_ref[...], preferred_element_type=F32)
                            + jnp.dot(lo, wh_ref[...], preferred_element_type=F32)
                            + jnp.dot(hi, wl_ref[...], preferred_element_type=F32))
        return carry

    lax.fori_loop(0, tm // ROUTER_ROWS, body, 0)

    lane = lax.broadcasted_iota(jnp.int32, (tm, LANES), 1).astype(F32)
    logits = jnp.where(lane < N_EXPERTS, logit_ref[...], -jnp.inf)
    m0 = jnp.max(logits, axis=-1, keepdims=True)
    e0 = jnp.min(jnp.where(logits == m0, lane, float(LANES)), axis=-1, keepdims=True)
    rest = jnp.where(lane == e0, -jnp.inf, logits)
    m1 = jnp.max(rest, axis=-1, keepdims=True)
    e1 = jnp.min(jnp.where(rest == m1, lane, float(LANES)), axis=-1, keepdims=True)
    t = jnp.exp(m1 - m0)
    w0 = 1.0 / (1.0 + t)
    w1 = t / (1.0 + t)

    onehot = jnp.where(jnp.logical_or(lane == e0, lane == e1), 1.0, 0.0)
    before = jnp.dot(tri_ref[...], onehot.astype(BF16), preferred_element_type=F32) + carry_ref[0:1, :]
    r0 = jnp.sum(jnp.where(lane == e0, before, 0.0), axis=-1, keepdims=True)
    r1 = jnp.sum(jnp.where(lane == e1, before, 0.0), axis=-1, keepdims=True)
    carry_ref[0:1, :] = carry_ref[0:1, :] + jnp.sum(onehot, axis=0, keepdims=True)

    meta = jnp.where(lane == 0, e0, jnp.where(lane == 1, e1, jnp.where(lane == 2, r0, jnp.where(lane == 3, r1, 0.0))))
    meta_ref[...] = meta.astype(jnp.int32)
    gate_ref[...] = jnp.where(lane == 0, w0, jnp.where(lane == 1, w1, 0.0))
    cnt_ref[...] = carry_ref[...]


def _router(x, g, w_router):
    T = x.shape[0]
    tm = min(1024, T)
    w_pad = jnp.zeros((D_MODEL, LANES), F32).at[:, :N_EXPERTS].set(w_router.astype(F32))
    wh = w_pad.astype(BF16)
    wl = (w_pad - wh.astype(F32)).astype(BF16)
    wspec = pl.BlockSpec((D_MODEL, LANES), lambda i: (0, 0))
    return pl.pallas_call(
        _router_kernel,
        out_shape=(jax.ShapeDtypeStruct((T, LANES), jnp.int32),
                   jax.ShapeDtypeStruct((T, LANES), F32),
                   jax.ShapeDtypeStruct((8, LANES), F32)),
        grid=(T // tm,),
        in_specs=[pl.BlockSpec((tm, D_MODEL), lambda i: (i, 0)), pl.BlockSpec((1, D_MODEL), lambda i: (0, 0)), wspec, wspec],
        out_specs=(pl.BlockSpec((tm, LANES), lambda i: (i, 0)),
                   pl.BlockSpec((tm, LANES), lambda i: (i, 0)),
                   pl.BlockSpec((8, LANES), lambda i: (0, 0))),
        scratch_shapes=[pltpu.VMEM((tm, tm), BF16), pltpu.VMEM((8, LANES), F32), pltpu.VMEM((tm, LANES), F32)],
        compiler_params=_params("arbitrary"),
        name="moe_router",
    )(x, g, wh, wl)


def _scatter_rows_kernel(dest_ref, x_hbm, init_hbm, o_hbm, sem, *, tc):
    del init_hbm
    base = pl.program_id(0) * tc

    def copy(t, k):
        d = dest_ref[0, 0, 2 * t + k]
        return pltpu.make_async_copy(x_hbm.at[pl.ds(base + t, 1)], o_hbm.at[pl.ds(d, 1)], sem)

    def issue(t, carry):
        copy(t, 0).start()
        copy(t, 1).start()
        return carry

    def drain(t, carry):
        copy(t, 0).wait()
        copy(t, 1).wait()
        return carry

    lax.fori_loop(0, tc, issue, 0)
    lax.fori_loop(0, tc, drain, 0)


def _scatter_rows(x, dest, n_rows):
    T = x.shape[0]
    tc = min(512, T)
    dest3 = dest.reshape(T // tc, 1, 2 * tc)
    init = jnp.zeros((n_rows, D_MODEL), x.dtype)
    return pl.pallas_call(
        functools.partial(_scatter_rows_kernel, tc=tc),
        out_shape=jax.ShapeDtypeStruct((n_rows, D_MODEL), x.dtype),
        grid=(T // tc,),
        in_specs=[
            pl.BlockSpec((1, 1, 2 * tc), lambda i: (i, 0, 0), memory_space=pltpu.SMEM),
            pl.BlockSpec(memory_space=pl.ANY),
            pl.BlockSpec(memory_space=pl.ANY),
        ],
        out_specs=pl.BlockSpec(memory_space=pl.ANY),
        scratch_shapes=[pltpu.SemaphoreType.DMA(())],
        input_output_aliases={2: 0},
        compiler_params=_params("arbitrary"),
        name="moe_scatter_rows",
    )(dest3, x, init)


def _combine_kernel(dest_ref, x_ref, gate_ref, y_hbm, o_ref, buf_ref, sem, *, tc):
    def copy(t, k):
        d = dest_ref[0, 0, 2 * t + k]
        return pltpu.make_async_copy(y_hbm.at[pl.ds(d, 1)], buf_ref.at[k, pl.ds(t, 1)], sem)

    def issue(t, carry):
        copy(t, 0).start()
        copy(t, 1).start()
        return carry

    def drain(t, carry):
        copy(t, 0).wait()
        copy(t, 1).wait()
        return carry

    lax.fori_loop(0, tc, issue, 0)
    lax.fori_loop(0, tc, drain, 0)
    w0 = gate_ref[:, 0:1]
    w1 = gate_ref[:, 1:2]
    o_ref[...] = x_ref[...] + w0 * buf_ref[0] + w1 * buf_ref[1]


def _combine(x, gate, y, dest):
    T = x.shape[0]
    tc = min(256, T)
    dest3 = dest.reshape(T // tc, 1, 2 * tc)
    return pl.pallas_call(
        functools.partial(_combine_kernel, tc=tc),
        out_shape=jax.ShapeDtypeStruct((T, D_MODEL), F32),
        grid=(T // tc,),
        in_specs=[
            pl.BlockSpec((1, 1, 2 * tc), lambda i: (i, 0, 0), memory_space=pltpu.SMEM),
            pl.BlockSpec((tc, D_MODEL), lambda i: (i, 0)),
            pl.BlockSpec((tc, LANES), lambda i: (i, 0)),
            pl.BlockSpec(memory_space=pl.ANY),
        ],
        out_specs=pl.BlockSpec((tc, D_MODEL), lambda i: (i, 0)),
        scratch_shapes=[pltpu.VMEM((2, tc, D_MODEL), F32), pltpu.SemaphoreType.DMA(())],
        compiler_params=_params("arbitrary"),
        name="moe_combine",
    )(dest3, x, gate, y)


def _moe(x, g, w_router, wg, wu, wd):
    T = x.shape[0]
    tm = min(1024, T)
    meta, gate, cnt = _router(x, g, w_router)
    expert = meta[:, 0:2]
    rank = meta[:, 2:4]
    counts = cnt[0, :N_EXPERTS].astype(jnp.int32)
    tiles_per = (counts + tm - 1) // tm
    tile_end = jnp.cumsum(tiles_per)
    offset = (tile_end - tiles_per) * tm
    dest = offset[expert] + rank
    n_tiles = (2 * T) // tm + N_EXPERTS
    tile_id = jnp.arange(n_tiles)
    tile_active = (tile_id < tile_end[-1]).astype(jnp.int32)
    tile_expert = jnp.searchsorted(tile_end, jnp.minimum(tile_id, tile_end[-1] - 1), side="right").astype(jnp.int32)
    xs = _scatter_rows(x, dest, n_tiles * tm)
    ys = _ffn(xs, g, wg, wu, wd, tile_expert, tile_active, residual=False, tm=tm)
    return _combine(x, gate, ys, dest)


def kernel(x, mix_norm_g, w_in, q_norm_g, k_norm_g, rel_bias, conv_dw_w, conv_dw_b, conv_ln_g, conv_ln_b,
           w_branch_attn, w_branch_conv, w_out, ffn_norm_g, dense_w_gate, dense_w_up, dense_w_down,
           moe_router, moe_w_gate, moe_w_up, moe_w_down):
    B, S, D = x.shape
    T = B * S
    depth = w_in.shape[0]
    qrows = 256
    conv_tile = 256
    row = lambda v: v.astype(F32)[None]
    bias_tab = _bias_table(rel_bias, qrows)
    xf = x.reshape(T, D).astype(F32)
    dense_tiles = jnp.zeros((T // min(1024, T),), jnp.int32)
    for l in range(depth):
        p = _in_proj(xf, row(mix_norm_g[l]), w_in[l].astype(BF16))
        a = _attention(p, bias_tab, q_norm_g[l], k_norm_g[l], B, S, qrows)
        c = _conv_branch(p, conv_dw_w[l], conv_dw_b[l], conv_ln_g[l], conv_ln_b[l], B, S, conv_tile)
        xf = _merge(p, a, c, xf, w_branch_attn[l].astype(BF16), w_branch_conv[l].astype(BF16), w_out[l].astype(BF16))
        i = l // 2
        if l % 2 == 0:
            xf = _ffn(xf, row(ffn_norm_g[l]), dense_w_gate[i][None].astype(BF16), dense_w_up[i][None].astype(BF16),
                      dense_w_down[i][None].astype(BF16), dense_tiles, dense_tiles + 1,
                      residual=True, tm=min(1024, T))
        else:
            xf = _moe(xf, row(ffn_norm_g[l]), moe_router[i], moe_w_gate[i].astype(BF16), moe_w_up[i].astype(BF16),
                      moe_w_down[i].astype(BF16))
    return xf.reshape(B, S, D).astype(x.dtype)
```

```python
import functools

import jax
import jax.numpy as jnp
from jax import lax
from jax.experimental import pallas as pl
from jax.experimental.pallas import tpu as pltpu

F32 = jnp.float32
BF16 = jnp.bfloat16

D_MODEL = 2048
CHUNK = 64
LEFT = 8 * CHUNK
N_HEADS = 16
HEAD_DIM = 64
D_ATTN = N_HEADS * HEAD_DIM
MAX_REL = 256
D_CONV = 1024
CONV_WIDTH = 31
D_FF = 7168
N_EXPERTS = 8
EPS = 1e-6
D_IN = 9216
COL_BLOCK = 1024

LANES = 128
V7X_VMEM_BYTES = 64 * 1024 * 1024
VMEM_LIMIT = V7X_VMEM_BYTES - 8 * 1024 * 1024

NEG_INF = -1e30
LOG2_E = 1.4426950408889634


def _params(*sem):
    return pltpu.CompilerParams(dimension_semantics=sem, vmem_limit_bytes=VMEM_LIMIT)


def _rms_rows_to(dst_ref, x_ref, g_ref, rows=64):
    def body(r, carry):
        sl = pl.ds(pl.multiple_of(r * rows, rows), rows)
        x = x_ref[sl, :]
        ms = jnp.mean(x * x, axis=-1, keepdims=True)
        dst_ref[sl, :] = (x * lax.rsqrt(ms + EPS) * g_ref[...]).astype(dst_ref.dtype)
        return carry

    lax.fori_loop(0, x_ref.shape[0] // rows, body, 0)


def _in_proj_kernel(x_ref, g_ref, w_ref, o_ref, h_ref):
    @pl.when(pl.program_id(1) == 0)
    def _():
        _rms_rows_to(h_ref, x_ref, g_ref)

    o_ref[...] = jnp.dot(h_ref[...], w_ref[...], preferred_element_type=F32).astype(o_ref.dtype)


def _in_proj(x, g, w, layer):
    T = x.shape[0]
    tm = min(1024, T)
    tn = COL_BLOCK
    return pl.pallas_call(
        _in_proj_kernel,
        out_shape=jax.ShapeDtypeStruct((T, D_IN), BF16),
        grid=(T // tm, D_IN // tn),
        in_specs=[
            pl.BlockSpec((tm, D_MODEL), lambda i, j: (i, 0)),
            pl.BlockSpec((1, D_MODEL), lambda i, j: (0, 0)),
            pl.BlockSpec((None, D_MODEL, tn), lambda i, j: (layer, 0, j)),
        ],
        out_specs=pl.BlockSpec((tm, tn), lambda i, j: (i, j)),
        scratch_shapes=[pltpu.VMEM((tm, D_MODEL), BF16)],
        compiler_params=_params("parallel", "arbitrary"),
        name="in_proj",
    )(x, g, w)


def _pair_rms_scale(xp):
    lane = lax.broadcasted_iota(jnp.int32, xp.shape, 1)
    lo = lane < HEAD_DIM
    sq = xp * xp
    s_lo = jnp.sum(jnp.where(lo, sq, 0.0), axis=-1, keepdims=True)
    s_hi = jnp.sum(jnp.where(lo, 0.0, sq), axis=-1, keepdims=True)
    ms = jnp.where(lo, s_lo, s_hi) * (1.0 / HEAD_DIM)
    return lax.rsqrt(ms + EPS)


def _attn_kernel(q_ref, k_ref, v_ref, bias_ref, gq_ref, gk_ref, o_ref, kn_ref, qn_ref, *, qrows, win):
    qb = pl.program_id(1)
    n_pairs = D_ATTN // LANES

    @pl.when(qb == 0)
    def _():
        rows = 128

        def body(r, carry):
            sl = pl.ds(pl.multiple_of(r * rows, rows), rows)
            for hp in range(n_pairs):
                cs = slice(hp * LANES, (hp + 1) * LANES)
                kp = k_ref[sl, cs].astype(F32)
                kn_ref[sl, cs] = (kp * _pair_rms_scale(kp) * gk_ref[...]).astype(BF16)
            return carry

        lax.fori_loop(0, k_ref.shape[0] // rows, body, 0)

    for hp in range(n_pairs):
        cs = slice(hp * LANES, (hp + 1) * LANES)
        qp = q_ref[:, cs].astype(F32)
        qn_ref[:, cs] = (qp * _pair_rms_scale(qp) * gq_ref[...]).astype(BF16)

    ws = pl.multiple_of(jnp.maximum(qb * qrows - LEFT, 0), CHUNK)
    lane = lax.broadcasted_iota(jnp.int32, (qrows, LANES), 1)
    lo = lane < HEAD_DIM
    for hp in range(n_pairs):
        cs = slice(hp * LANES, (hp + 1) * LANES)
        kw = kn_ref[pl.ds(ws, win), cs]
        vw = v_ref[pl.ds(ws, win), cs]
        qp = qn_ref[:, cs]
        outs = []
        for j in range(2):
            qm = jnp.where(lo if j == 0 else jnp.logical_not(lo), qp, jnp.zeros_like(qp))
            s = lax.dot_general(qm, kw, (((1,), (1,)), ((), ())), preferred_element_type=F32)
            s = s + bias_ref[2 * hp + j]
            m = jnp.max(s, axis=-1, keepdims=True)
            e = jnp.exp2(s - m)
            l = jnp.sum(e, axis=-1, keepdims=True)
            o = jnp.dot(e.astype(BF16), vw, preferred_element_type=F32)
            outs.append(o / l)
        o_ref[:, cs] = jnp.where(lo, outs[0], outs[1]).astype(o_ref.dtype)


def _bias_table(rel_bias, qrows):
    win = LEFT + qrows
    n_var = LEFT // qrows + 1
    period = win + qrows
    diag = jnp.concatenate([jnp.arange(win), jnp.arange(-qrows, 0)])
    off = (jnp.arange(n_var) * qrows)[:, None]
    rel = jnp.clip(off - diag[None, :], -MAX_REL, MAX_REL) + MAX_REL
    vec = jnp.transpose(rel_bias.astype(F32)[:, rel], (1, 0, 2))
    flat = jnp.tile(vec, (1, 1, qrows))[..., :qrows * (period - 1)]
    bias = flat.reshape(n_var, N_HEADS, qrows, period - 1)[..., :win]
    qi = off[:, :, None] + jnp.arange(qrows)[None, :, None]
    kj = jnp.arange(win)[None, None, :]
    dchunk = qi // CHUNK - kj // CHUNK
    valid = (dchunk >= 0) & (dchunk <= LEFT // CHUNK)
    return jnp.where(valid[:, None], bias * LOG2_E, NEG_INF)


def _attention(p, bias_tab, gq, gk, B, S, qrows):
    T = B * S
    win = LEFT + qrows
    nq = S // qrows
    n_var = bias_tab.shape[0]
    scale = HEAD_DIM ** -0.5
    gq2 = (jnp.tile(gq.astype(F32), 2) * (scale * LOG2_E))[None]
    gk2 = jnp.tile(gk.astype(F32), 2)[None]
    kern = functools.partial(_attn_kernel, qrows=qrows, win=win)
    return pl.pallas_call(
        kern,
        out_shape=jax.ShapeDtypeStruct((T, D_ATTN), BF16),
        grid=(B, nq),
        in_specs=[
            pl.BlockSpec((qrows, COL_BLOCK), lambda b, q: (b * nq + q, 0)),
            pl.BlockSpec((S, COL_BLOCK), lambda b, q: (b, 1), pipeline_mode=pl.Buffered(1)),
            pl.BlockSpec((S, COL_BLOCK), lambda b, q: (b, 2), pipeline_mode=pl.Buffered(1)),
            pl.BlockSpec((None, N_HEADS, qrows, win), lambda b, q: (jnp.minimum(q, n_var - 1), 0, 0, 0)),
            pl.BlockSpec((1, LANES), lambda b, q: (0, 0)),
            pl.BlockSpec((1, LANES), lambda b, q: (0, 0)),
        ],
        out_specs=pl.BlockSpec((qrows, D_ATTN), lambda b, q: (b * nq + q, 0)),
        scratch_shapes=[pltpu.VMEM((S, D_ATTN), BF16), pltpu.VMEM((qrows, D_ATTN), BF16)],
        compiler_params=_params("parallel", "arbitrary"),
        name="band_attention",
    )(p, p, p, bias_tab, gq2, gk2)


CONV_HIST = 32
SUBLANES = 8
CONV_ROWS = 2 * SUBLANES
CONV_SHIFT_ROWS = CONV_HIST - SUBLANES


def _conv_kernel(a_ref, g_ref, w_ref, b_ref, lg_ref, lb_ref, o_ref, y_ref, ys_ref, *, tt):
    t = pl.program_id(1)

    @pl.when(t == 0)
    def _():
        y_ref[0:CONV_HIST, :] = jnp.zeros((CONV_HIST, D_CONV), F32)

    @pl.when(t > 0)
    def _():
        y_ref[0:CONV_HIST, :] = y_ref[tt:tt + CONV_HIST, :]

    a = a_ref[...].astype(F32)
    g = g_ref[...].astype(F32)
    y_ref[CONV_HIST:CONV_HIST + tt, :] = a * jax.nn.sigmoid(g)

    n_shift = tt + CONV_SHIFT_ROWS
    for r in range(1, SUBLANES):
        ys_ref[r - 1] = y_ref[r:r + n_shift, :]

    shift = CONV_HIST - (CONV_WIDTH - 1)
    halves = CONV_ROWS // SUBLANES
    for rb in range(tt // CONV_ROWS):
        r0 = rb * CONV_ROWS
        acc = [b_ref[...] for _ in range(halves)]
        for j in range(CONV_WIDTH):
            a8, r = divmod(shift + j, SUBLANES)
            src = y_ref if r == 0 else ys_ref.at[r - 1]
            wj = w_ref[j]
            for hf in range(halves):
                m0 = r0 + (a8 + hf) * SUBLANES
                acc[hf] = acc[hf] + wj * src[m0:m0 + SUBLANES, :]
        for hf in range(halves):
            mu = jnp.mean(acc[hf], axis=-1, keepdims=True)
            xc = acc[hf] - mu
            var = jnp.mean(xc * xc, axis=-1, keepdims=True)
            z = xc * lax.rsqrt(var + EPS) * lg_ref[...] + lb_ref[...]
            rows = slice(r0 + hf * SUBLANES, r0 + (hf + 1) * SUBLANES)
            o_ref[rows, :] = (z * jax.nn.sigmoid(z)).astype(o_ref.dtype)


def _conv_branch(p, w_dw, b_dw, ln_g, ln_b, B, S, tt):
    T = B * S
    nt = S // tt
    rep = lambda v: jnp.broadcast_to(v.astype(F32)[..., None, :], v.shape[:-1] + (SUBLANES, D_CONV))
    vec = pl.BlockSpec((SUBLANES, D_CONV), lambda b, t: (0, 0))
    return pl.pallas_call(
        functools.partial(_conv_kernel, tt=tt),
        out_shape=jax.ShapeDtypeStruct((T, D_CONV), BF16),
        grid=(B, nt),
        in_specs=[
            pl.BlockSpec((tt, COL_BLOCK), lambda b, t: (b * nt + t, 3)),
            pl.BlockSpec((tt, COL_BLOCK), lambda b, t: (b * nt + t, 4)),
            pl.BlockSpec((CONV_WIDTH, SUBLANES, D_CONV), lambda b, t: (0, 0, 0)),
            vec, vec, vec,
        ],
        out_specs=pl.BlockSpec((tt, D_CONV), lambda b, t: (b * nt + t, 0)),
        scratch_shapes=[pltpu.VMEM((CONV_HIST + tt, D_CONV), F32),
                        pltpu.VMEM((SUBLANES - 1, tt + CONV_SHIFT_ROWS, D_CONV), F32)],
        compiler_params=_params("parallel", "arbitrary"),
        name="conv_branch",
    )(p, p, rep(w_dw), rep(b_dw), rep(ln_g), rep(ln_b))


MERGE_COLS = 512


def _merge_kernel(a_ref, c_ref, ga0_ref, ga1_ref, gc0_ref, gc1_ref, x_ref, wa_ref, wb_ref, wo_ref, o_ref, m_ref):
    ga_refs = (ga0_ref, ga1_ref)
    gc_refs = (gc0_ref, gc1_ref)
    per_block = COL_BLOCK // MERGE_COLS
    for n in range(D_MODEL // MERGE_COLS):
        cs = slice(n * MERGE_COLS, (n + 1) * MERGE_COLS)
        gs = slice((n % per_block) * MERGE_COLS, (n % per_block + 1) * MERGE_COLS)
        ya = jnp.dot(a_ref[...], wa_ref[:, cs], preferred_element_type=F32)
        yc = jnp.dot(c_ref[...], wb_ref[:, cs], preferred_element_type=F32)
        ga = jax.nn.sigmoid(ga_refs[n // per_block][:, gs].astype(F32))
        gc = jax.nn.sigmoid(gc_refs[n // per_block][:, gs].astype(F32))
        m_ref[:, cs] = (ga * ya + gc * yc).astype(BF16)
    for n in range(D_MODEL // MERGE_COLS):
        cs = slice(n * MERGE_COLS, (n + 1) * MERGE_COLS)
        o_ref[:, cs] = x_ref[:, cs] + jnp.dot(m_ref[...], wo_ref[:, cs], preferred_element_type=F32)


def _merge(p, a, c, x, wa, wb, wo, layer):
    T = x.shape[0]
    tm = min(512, T)
    gate = lambda blk: pl.BlockSpec((tm, COL_BLOCK), lambda i: (i, blk))
    resident = lambda shape: pl.BlockSpec((None,) + shape, lambda i: (layer, 0, 0), pipeline_mode=pl.Buffered(1))
    return pl.pallas_call(
        _merge_kernel,
        out_shape=jax.ShapeDtypeStruct((T, D_MODEL), F32),
        grid=(T // tm,),
        in_specs=[
            pl.BlockSpec((tm, D_ATTN), lambda i: (i, 0)),
            pl.BlockSpec((tm, D_CONV), lambda i: (i, 0)),
            gate(5), gate(6), gate(7), gate(8),
            pl.BlockSpec((tm, D_MODEL), lambda i: (i, 0)),
            resident((D_ATTN, D_MODEL)),
            resident((D_CONV, D_MODEL)),
            resident((D_MODEL, D_MODEL)),
        ],
        out_specs=pl.BlockSpec((tm, D_MODEL), lambda i: (i, 0)),
        scratch_shapes=[pltpu.VMEM((tm, D_MODEL), BF16)],
        compiler_params=_params("parallel"),
        name="merge_out_proj",
    )(a, c, p, p, p, p, x, wa, wb, wo)


def _ffn_kernel(te_ref, ta_ref, x_ref, g_ref, wg_ref, wu_ref, wd_ref, o_ref, h_ref, *, residual):
    del te_ref
    i = pl.program_id(0)
    f = pl.program_id(1)

    @pl.when(f == 0)
    def _():
        _rms_rows_to(h_ref, x_ref, g_ref)
        if residual:
            o_ref[...] = x_ref[...]
        else:
            o_ref[...] = jnp.zeros(o_ref.shape, o_ref.dtype)

    @pl.when(ta_ref[i] == 1)
    def _():
        h = h_ref[...]
        gt = jnp.dot(h, wg_ref[...], preferred_element_type=F32)
        up = jnp.dot(h, wu_ref[...], preferred_element_type=F32)
        act = (gt * jax.nn.sigmoid(gt) * up).astype(BF16)
        o_ref[...] += jnp.dot(act, wd_ref[...], preferred_element_type=F32)


def _ffn(x, g, wg, wu, wd, layer, tile_expert, tile_active, *, residual, tm):
    R = x.shape[0]
    tf = 512
    nf = D_FF // tf
    fsel = lambda i, f, te, ta: jnp.where(ta[i] == 1, f, nf - 1)
    up_spec = pl.BlockSpec((None, None, D_MODEL, tf), lambda i, f, te, ta: (layer, te[i], 0, fsel(i, f, te, ta)))
    down_spec = pl.BlockSpec((None, None, tf, D_MODEL), lambda i, f, te, ta: (layer, te[i], fsel(i, f, te, ta), 0))
    grid_spec = pltpu.PrefetchScalarGridSpec(
        num_scalar_prefetch=2,
        grid=(R // tm, nf),
        in_specs=[
            pl.BlockSpec((tm, D_MODEL), lambda i, f, te, ta: (i, 0)),
            pl.BlockSpec((1, D_MODEL), lambda i, f, te, ta: (0, 0)),
            up_spec, up_spec, down_spec,
        ],
        out_specs=pl.BlockSpec((tm, D_MODEL), lambda i, f, te, ta: (i, 0)),
        scratch_shapes=[pltpu.VMEM((tm, D_MODEL), BF16)],
    )
    return pl.pallas_call(
        functools.partial(_ffn_kernel, residual=residual),
        out_shape=jax.ShapeDtypeStruct((R, D_MODEL), F32),
        grid_spec=grid_spec,
        compiler_params=_params("parallel", "arbitrary"),
        name="swiglu_residual" if residual else "swiglu_grouped",
    )(tile_expert, tile_active, x, g, wg, wu, wd)


ROUTER_ROWS = 128


def _router_kernel(x_ref, g_ref, wh_ref, wl_ref, meta_ref, gate_ref, cnt_ref, tri_ref, carry_ref, logit_ref):
    i = pl.program_id(0)
    tm = x_ref.shape[0]

    @pl.when(i == 0)
    def _():
        r = lax.broadcasted_iota(jnp.int32, (tm, tm), 0)
        c = lax.broadcasted_iota(jnp.int32, (tm, tm), 1)
        tri_ref[...] = jnp.where(c < r, 1.0, 0.0).astype(BF16)
        carry_ref[...] = jnp.zeros(carry_ref.shape, F32)

    def body(r, carry):
        sl = pl.ds(pl.multiple_of(r * ROUTER_ROWS, ROUTER_ROWS), ROUTER_ROWS)
        x = x_ref[sl, :]
        ms = jnp.mean(x * x, axis=-1, keepdims=True)
        h = x * lax.rsqrt(ms + EPS) * g_ref[...]
        hi = h.astype(BF16)
        lo = (h - hi.astype(F32)).astype(BF16)
        logit_ref[sl, :] = (jnp.dot(hi, wh_ref[...], preferred_element_type=F32)
                            + jnp.dot(lo, wh_ref[...], preferred_element_type=F32)
                            + jnp.dot(hi, wl_ref[...], preferred_element_type=F32))
        return carry

    lax.fori_loop(0, tm // ROUTER_ROWS, body, 0)

    lane = lax.broadcasted_iota(jnp.int32, (tm, LANES), 1).astype(F32)
    logits = jnp.where(lane < N_EXPERTS, logit_ref[...], -jnp.inf)
    m0 = jnp.max(logits, axis=-1, keepdims=True)
    e0 = jnp.min(jnp.where(logits == m0, lane, float(LANES)), axis=-1, keepdims=True)
    rest = jnp.where(lane == e0, -jnp.inf, logits)
    m1 = jnp.max(rest, axis=-1, keepdims=True)
    e1 = jnp.min(jnp.where(rest == m1, lane, float(LANES)), axis=-1, keepdims=True)
    t = jnp.exp(m1 - m0)
    w0 = 1.0 / (1.0 + t)
    w1 = t / (1.0 + t)

    onehot = jnp.where(jnp.logical_or(lane == e0, lane == e1), 1.0, 0.0)
    before = jnp.dot(tri_ref[...], onehot.astype(BF16), preferred_element_type=F32) + carry_ref[0:1, :]
    r0 = jnp.sum(jnp.where(lane == e0, before, 0.0), axis=-1, keepdims=True)
    r1 = jnp.sum(jnp.where(lane == e1, before, 0.0), axis=-1, keepdims=True)
    carry_ref[0:1, :] = carry_ref[0:1, :] + jnp.sum(onehot, axis=0, keepdims=True)

    meta = jnp.where(lane == 0, e0, jnp.where(lane == 1, e1, jnp.where(lane == 2, r0, jnp.where(lane == 3, r1, 0.0))))
    meta_ref[...] = meta.astype(jnp.int32)
    gate_ref[...] = jnp.where(lane == 0, w0, jnp.where(lane == 1, w1, 0.0))
    cnt_ref[...] = carry_ref[...]


def _router(x, g, w_router):
    T = x.shape[0]
    tm = min(1024, T)
    w_pad = jnp.zeros((D_MODEL, LANES), F32).at[:, :N_EXPERTS].set(w_router.astype(F32))
    wh = w_pad.astype(BF16)
    wl = (w_pad - wh.astype(F32)).astype(BF16)
    wspec = pl.BlockSpec((D_MODEL, LANES), lambda i: (0, 0))
    return pl.pallas_call(
        _router_kernel,
        out_shape=(jax.ShapeDtypeStruct((T, LANES), jnp.int32),
                   jax.ShapeDtypeStruct((T, LANES), F32),
                   jax.ShapeDtypeStruct((8, LANES), F32)),
        grid=(T // tm,),
        in_specs=[pl.BlockSpec((tm, D_MODEL), lambda i: (i, 0)), pl.BlockSpec((1, D_MODEL), lambda i: (0, 0)), wspec, wspec],
        out_specs=(pl.BlockSpec((tm, LANES), lambda i: (i, 0)),
                   pl.BlockSpec((tm, LANES), lambda i: (i, 0)),
                   pl.BlockSpec((8, LANES), lambda i: (0, 0))),
        scratch_shapes=[pltpu.VMEM((tm, tm), BF16), pltpu.VMEM((8, LANES), F32), pltpu.VMEM((tm, LANES), F32)],
        compiler_params=_params("arbitrary"),
        name="moe_router",
    )(x, g, wh, wl)


ROW_DMA_UNROLL = 8


def _scatter_rows_kernel(dest_ref, x_ref, init_hbm, o_hbm, sem, *, tc):
    del init_hbm

    def copy(t, k):
        d = dest_ref[0, 0, 2 * t + k]
        return pltpu.make_async_copy(x_ref.at[pl.ds(t, 1)], o_hbm.at[pl.ds(d, 1)], sem)

    def issue(t, carry):
        copy(t, 0).start()
        copy(t, 1).start()
        return carry

    def drain(t, carry):
        copy(t, 0).wait()
        copy(t, 1).wait()
        return carry

    lax.fori_loop(0, tc, issue, 0, unroll=ROW_DMA_UNROLL)
    lax.fori_loop(0, tc, drain, 0, unroll=ROW_DMA_UNROLL)


def _scatter_rows(x, dest, n_rows):
    T = x.shape[0]
    tc = min(256, T)
    dest3 = dest.reshape(T // tc, 1, 2 * tc)
    init = jnp.zeros((n_rows, D_MODEL), x.dtype)
    return pl.pallas_call(
        functools.partial(_scatter_rows_kernel, tc=tc),
        out_shape=jax.ShapeDtypeStruct((n_rows, D_MODEL), x.dtype),
        grid=(T // tc,),
        in_specs=[
            pl.BlockSpec((1, 1, 2 * tc), lambda i: (i, 0, 0), memory_space=pltpu.SMEM),
            pl.BlockSpec((tc, D_MODEL), lambda i: (i, 0)),
            pl.BlockSpec(memory_space=pl.ANY),
        ],
        out_specs=pl.BlockSpec(memory_space=pl.ANY),
        scratch_shapes=[pltpu.SemaphoreType.DMA(())],
        input_output_aliases={2: 0},
        compiler_params=_params("arbitrary"),
        name="moe_scatter_rows",
    )(dest3, x, init)


def _combine_kernel(dest_ref, x_ref, gate_ref, y_hbm, o_ref, buf_ref, sem, *, tc):
    def copy(t, k):
        d = dest_ref[0, 0, 2 * t + k]
        return pltpu.make_async_copy(y_hbm.at[pl.ds(d, 1)], buf_ref.at[k, pl.ds(t, 1)], sem)

    def issue(t, carry):
        copy(t, 0).start()
        copy(t, 1).start()
        return carry

    def drain(t, carry):
        copy(t, 0).wait()
        copy(t, 1).wait()
        return carry

    lax.fori_loop(0, tc, issue, 0, unroll=ROW_DMA_UNROLL)
    lax.fori_loop(0, tc, drain, 0, unroll=ROW_DMA_UNROLL)
    w0 = gate_ref[:, 0:1]
    w1 = gate_ref[:, 1:2]
    o_ref[...] = x_ref[...] + w0 * buf_ref[0] + w1 * buf_ref[1]


def _combine(x, gate, y, dest):
    T = x.shape[0]
    tc = min(256, T)
    dest3 = dest.reshape(T // tc, 1, 2 * tc)
    return pl.pallas_call(
        functools.partial(_combine_kernel, tc=tc),
        out_shape=jax.ShapeDtypeStruct((T, D_MODEL), F32),
        grid=(T // tc,),
        in_specs=[
            pl.BlockSpec((1, 1, 2 * tc), lambda i: (i, 0, 0), memory_space=pltpu.SMEM),
            pl.BlockSpec((tc, D_MODEL), lambda i: (i, 0)),
            pl.BlockSpec((tc, LANES), lambda i: (i, 0)),
            pl.BlockSpec(memory_space=pl.ANY),
        ],
        out_specs=pl.BlockSpec((tc, D_MODEL), lambda i: (i, 0)),
        scratch_shapes=[pltpu.VMEM((2, tc, D_MODEL), F32), pltpu.SemaphoreType.DMA(())],
        compiler_params=_params("arbitrary"),
        name="moe_combine",
    )(dest3, x, gate, y)


def _moe(x, g, w_router, wg, wu, wd, layer):
    T = x.shape[0]
    tm = min(1024, T)
    meta, gate, cnt = _router(x, g, w_router)
    expert = meta[:, 0:2]
    rank = meta[:, 2:4]
    counts = cnt[0, :N_EXPERTS].astype(jnp.int32)
    tiles_per = (counts + tm - 1) // tm
    tile_end = jnp.cumsum(tiles_per)
    offset = (tile_end - tiles_per) * tm
    dest = offset[expert] + rank
    n_tiles = (2 * T) // tm + N_EXPERTS
    tile_id = jnp.arange(n_tiles)
    tile_active = (tile_id < tile_end[-1]).astype(jnp.int32)
    tile_expert = jnp.searchsorted(tile_end, jnp.minimum(tile_id, tile_end[-1] - 1), side="right").astype(jnp.int32)
    xs = _scatter_rows(x, dest, n_tiles * tm)
    ys = _ffn(xs, g, wg, wu, wd, layer, tile_expert, tile_active, residual=False, tm=tm)
    return _combine(x, gate, ys, dest)


def kernel(x, mix_norm_g, w_in, q_norm_g, k_norm_g, rel_bias, conv_dw_w, conv_dw_b, conv_ln_g, conv_ln_b,
           w_branch_attn, w_branch_conv, w_out, ffn_norm_g, dense_w_gate, dense_w_up, dense_w_down,
           moe_router, moe_w_gate, moe_w_up, moe_w_down):
    B, S, D = x.shape
    T = B * S
    depth = w_in.shape[0]
    qrows = 256
    conv_tile = 256
    row = lambda v: v.astype(F32)[None]
    bias_tab = _bias_table(rel_bias, qrows)
    xf = x.reshape(T, D).astype(F32)
    dense_tiles = jnp.zeros((T // min(1024, T),), jnp.int32)
    cast = lambda w: w.astype(BF16)
    w_in_b, wa_b, wb_b, wo_b = cast(w_in), cast(w_branch_attn), cast(w_branch_conv), cast(w_out)
    dense_b = [cast(w)[:, None] for w in (dense_w_gate, dense_w_up, dense_w_down)]
    moe_b = [cast(w) for w in (moe_w_gate, moe_w_up, moe_w_down)]
    for l in range(depth):
        p = _in_proj(xf, row(mix_norm_g[l]), w_in_b, l)
        a = _attention(p, bias_tab, q_norm_g[l], k_norm_g[l], B, S, qrows)
        c = _conv_branch(p, conv_dw_w[l], conv_dw_b[l], conv_ln_g[l], conv_ln_b[l], B, S, conv_tile)
        xf = _merge(p, a, c, xf, wa_b, wb_b, wo_b, l)
        i = l // 2
        if l % 2 == 0:
            xf = _ffn(xf, row(ffn_norm_g[l]), *dense_b, i, dense_tiles, dense_tiles + 1,
                      residual=True, tm=min(1024, T))
        else:
            xf = _moe(xf, row(ffn_norm_g[l]), moe_router[i], *moe_b, i)
    return xf.reshape(B, S, D).astype(x.dtype)
```

```python
import functools

import jax
import jax.numpy as jnp
from jax import lax
from jax.experimental import pallas as pl
from jax.experimental.pallas import tpu as pltpu

F32 = jnp.float32
BF16 = jnp.bfloat16

D_MODEL = 2048
CHUNK = 64
LEFT = 8 * CHUNK
N_HEADS = 16
HEAD_DIM = 64
D_ATTN = N_HEADS * HEAD_DIM
MAX_REL = 256
D_CONV = 1024
CONV_WIDTH = 31
D_FF = 7168
N_EXPERTS = 8
EPS = 1e-6
D_IN = 9216
COL_BLOCK = 1024

LANES = 128
V7X_VMEM_BYTES = 64 * 1024 * 1024
VMEM_LIMIT = V7X_VMEM_BYTES - 8 * 1024 * 1024

NEG_INF = -1e30
LOG2_E = 1.4426950408889634


def _params(*sem):
    return pltpu.CompilerParams(dimension_semantics=sem, vmem_limit_bytes=VMEM_LIMIT)


def _rms_rows_to(dst_ref, x_ref, g_ref, rows=64):
    def body(r, carry):
        sl = pl.ds(pl.multiple_of(r * rows, rows), rows)
        x = x_ref[sl, :]
        ms = jnp.mean(x * x, axis=-1, keepdims=True)
        dst_ref[sl, :] = (x * lax.rsqrt(ms + EPS) * g_ref[...]).astype(dst_ref.dtype)
        return carry

    lax.fori_loop(0, x_ref.shape[0] // rows, body, 0, unroll=4)


CAST_BLOCK_BYTES = 8 * 1024 * 1024


def _cast_kernel(x_ref, o_ref):
    o_ref[...] = x_ref[...].astype(o_ref.dtype)


def _cast_bf16(w):
    C = w.shape[-1]
    flat = w.reshape(-1, C)
    rows = min(flat.shape[0], pl.next_power_of_2(CAST_BLOCK_BYTES // (C * 4) + 1) // 2)
    assert flat.shape[0] % rows == 0 and rows % 16 == 0
    out = pl.pallas_call(
        _cast_kernel,
        out_shape=jax.ShapeDtypeStruct(flat.shape, BF16),
        grid=(flat.shape[0] // rows,),
        in_specs=[pl.BlockSpec((rows, C), lambda i: (i, 0))],
        out_specs=pl.BlockSpec((rows, C), lambda i: (i, 0)),
        compiler_params=_params("parallel"),
        name="cast_bf16",
    )(flat)
    return out.reshape(w.shape)


def _in_proj_kernel(x_ref, g_ref, w_ref, o_ref, h_ref):
    @pl.when(pl.program_id(1) == 0)
    def _():
        _rms_rows_to(h_ref, x_ref, g_ref)

    o_ref[...] = jnp.dot(h_ref[...], w_ref[...], preferred_element_type=F32).astype(o_ref.dtype)


def _in_proj(x, g, w, layer):
    T = x.shape[0]
    tm = min(1024, T)
    tn = COL_BLOCK
    return pl.pallas_call(
        _in_proj_kernel,
        out_shape=jax.ShapeDtypeStruct((T, D_IN), BF16),
        grid=(T // tm, D_IN // tn),
        in_specs=[
            pl.BlockSpec((tm, D_MODEL), lambda i, j: (i, 0)),
            pl.BlockSpec((1, D_MODEL), lambda i, j: (0, 0)),
            pl.BlockSpec((None, D_MODEL, tn), lambda i, j: (layer, 0, j)),
        ],
        out_specs=pl.BlockSpec((tm, tn), lambda i, j: (i, j)),
        scratch_shapes=[pltpu.VMEM((tm, D_MODEL), BF16)],
        compiler_params=_params("parallel", "arbitrary"),
        name="in_proj",
    )(x, g, w)


def _pair_rms_scale(xp, ones_ref):
    ss = jnp.dot((xp * xp).astype(BF16), ones_ref[...], preferred_element_type=F32)
    return lax.rsqrt(ss * (1.0 / HEAD_DIM) + EPS)


def _attn_kernel(q_ref, k_ref, v_ref, bias_ref, gq_ref, gk_ref, ones_ref, o_ref, kn_ref, qn_ref, *, qrows, win):
    qb = pl.program_id(1)
    n_pairs = D_ATTN // LANES

    @pl.when(qb == 0)
    def _():
        rows = 128

        def body(r, carry):
            sl = pl.ds(pl.multiple_of(r * rows, rows), rows)
            for hp in range(n_pairs):
                cs = slice(hp * LANES, (hp + 1) * LANES)
                kp = k_ref[sl, cs].astype(F32)
                kn_ref[sl, cs] = (kp * _pair_rms_scale(kp, ones_ref) * gk_ref[...]).astype(BF16)
            return carry

        lax.fori_loop(0, k_ref.shape[0] // rows, body, 0)

    lane = lax.broadcasted_iota(jnp.int32, (qrows, LANES), 1)
    lo = lane < HEAD_DIM
    for hp in range(n_pairs):
        cs = slice(hp * LANES, (hp + 1) * LANES)
        qp = q_ref[:, cs].astype(F32)
        qn = qp * _pair_rms_scale(qp, ones_ref) * gq_ref[...]
        qn_ref[hp, 0:qrows, :] = jnp.where(lo, qn, 0.0).astype(BF16)
        qn_ref[hp, qrows:2 * qrows, :] = jnp.where(lo, 0.0, qn).astype(BF16)

    ws = pl.multiple_of(jnp.maximum(qb * qrows - LEFT, 0), CHUNK)
    for hp in range(n_pairs):
        cs = slice(hp * LANES, (hp + 1) * LANES)
        kw = kn_ref[pl.ds(ws, win), cs]
        vw = v_ref[pl.ds(ws, win), cs]
        s = lax.dot_general(qn_ref[hp], kw, (((1,), (1,)), ((), ())), preferred_element_type=F32)
        s = s + bias_ref[2 * hp:2 * hp + 2].reshape(2 * qrows, win)
        m = jnp.max(s, axis=-1, keepdims=True)
        e = jnp.exp2(s - m)
        vx = jnp.concatenate([vw, jnp.ones_like(vw)], axis=1)
        ox = jnp.dot(e.astype(BF16), vx, preferred_element_type=F32)
        o = ox[:, 0:LANES] / ox[:, LANES:2 * LANES]
        o_ref[:, cs] = jnp.where(lo, o[0:qrows], o[qrows:2 * qrows]).astype(o_ref.dtype)


def _bias_table(rel_bias, qrows):
    win = LEFT + qrows
    n_var = LEFT // qrows + 1
    period = win + qrows
    diag = jnp.concatenate([jnp.arange(win), jnp.arange(-qrows, 0)])
    off = (jnp.arange(n_var) * qrows)[:, None]
    rel = jnp.clip(off - diag[None, :], -MAX_REL, MAX_REL) + MAX_REL
    vec = jnp.transpose(rel_bias.astype(F32)[:, rel], (1, 0, 2))
    flat = jnp.tile(vec, (1, 1, qrows))[..., :qrows * (period - 1)]
    bias = flat.reshape(n_var, N_HEADS, qrows, period - 1)[..., :win]
    qi = off[:, :, None] + jnp.arange(qrows)[None, :, None]
    kj = jnp.arange(win)[None, None, :]
    dchunk = qi // CHUNK - kj // CHUNK
    valid = (dchunk >= 0) & (dchunk <= LEFT // CHUNK)
    return jnp.where(valid[:, None], bias * LOG2_E, NEG_INF)


def _attention(p, bias_tab, gq, gk, B, S, qrows):
    T = B * S
    win = LEFT + qrows
    nq = S // qrows
    n_var = bias_tab.shape[0]
    scale = HEAD_DIM ** -0.5
    gq2 = (jnp.tile(gq.astype(F32), 2) * (scale * LOG2_E))[None]
    gk2 = jnp.tile(gk.astype(F32), 2)[None]
    head_id = jnp.arange(LANES) // HEAD_DIM
    head_ones = (head_id[:, None] == head_id[None, :]).astype(BF16)
    kern = functools.partial(_attn_kernel, qrows=qrows, win=win)
    return pl.pallas_call(
        kern,
        out_shape=jax.ShapeDtypeStruct((T, D_ATTN), BF16),
        grid=(B, nq),
        in_specs=[
            pl.BlockSpec((qrows, COL_BLOCK), lambda b, q: (b * nq + q, 0)),
            pl.BlockSpec((S, COL_BLOCK), lambda b, q: (b, 1), pipeline_mode=pl.Buffered(1)),
            pl.BlockSpec((S, COL_BLOCK), lambda b, q: (b, 2)),
            pl.BlockSpec((None, N_HEADS, qrows, win), lambda b, q: (jnp.minimum(q, n_var - 1), 0, 0, 0)),
            pl.BlockSpec((1, LANES), lambda b, q: (0, 0)),
            pl.BlockSpec((1, LANES), lambda b, q: (0, 0)),
            pl.BlockSpec((LANES, LANES), lambda b, q: (0, 0)),
        ],
        out_specs=pl.BlockSpec((qrows, D_ATTN), lambda b, q: (b * nq + q, 0)),
        scratch_shapes=[pltpu.VMEM((S, D_ATTN), BF16), pltpu.VMEM((D_ATTN // LANES, 2 * qrows, LANES), BF16)],
        compiler_params=_params("parallel", "arbitrary"),
        name="band_attention",
    )(p, p, p, bias_tab, gq2, gk2, head_ones)


CONV_HIST = 32
SUBLANES = 8
CONV_ROWS = 2 * SUBLANES
CONV_SHIFT_ROWS = CONV_HIST - SUBLANES


def _conv_kernel(a_ref, g_ref, w_ref, b_ref, lg_ref, lb_ref, o_ref, y_ref, ys_ref, *, tt):
    t = pl.program_id(1)

    @pl.when(t == 0)
    def _():
        y_ref[0:CONV_HIST, :] = jnp.zeros((CONV_HIST, D_CONV), F32)

    @pl.when(t > 0)
    def _():
        y_ref[0:CONV_HIST, :] = y_ref[tt:tt + CONV_HIST, :]

    a = a_ref[...].astype(F32)
    g = g_ref[...].astype(F32)
    y_ref[CONV_HIST:CONV_HIST + tt, :] = a * jax.nn.sigmoid(g)

    n_shift = tt + CONV_SHIFT_ROWS
    for r in range(1, SUBLANES):
        ys_ref[r - 1] = y_ref[r:r + n_shift, :]

    shift = CONV_HIST - (CONV_WIDTH - 1)
    halves = CONV_ROWS // SUBLANES
    for rb in range(tt // CONV_ROWS):
        r0 = rb * CONV_ROWS
        acc = [b_ref[...] for _ in range(halves)]
        for j in range(CONV_WIDTH):
            a8, r = divmod(shift + j, SUBLANES)
            src = y_ref if r == 0 else ys_ref.at[r - 1]
            wj = w_ref[j]
            for hf in range(halves):
                m0 = r0 + (a8 + hf) * SUBLANES
                acc[hf] = acc[hf] + wj * src[m0:m0 + SUBLANES, :]
        for hf in range(halves):
            mu = jnp.mean(acc[hf], axis=-1, keepdims=True)
            xc = acc[hf] - mu
            var = jnp.mean(xc * xc, axis=-1, keepdims=True)
            z = xc * lax.rsqrt(var + EPS) * lg_ref[...] + lb_ref[...]
            rows = slice(r0 + hf * SUBLANES, r0 + (hf + 1) * SUBLANES)
            o_ref[rows, :] = (z * jax.nn.sigmoid(z)).astype(o_ref.dtype)


def _conv_branch(p, w_dw, b_dw, ln_g, ln_b, B, S, tt):
    T = B * S
    nt = S // tt
    rep = lambda v: jnp.broadcast_to(v.astype(F32)[..., None, :], v.shape[:-1] + (SUBLANES, D_CONV))
    vec = pl.BlockSpec((SUBLANES, D_CONV), lambda b, t: (0, 0))
    return pl.pallas_call(
        functools.partial(_conv_kernel, tt=tt),
        out_shape=jax.ShapeDtypeStruct((T, D_CONV), BF16),
        grid=(B, nt),
        in_specs=[
            pl.BlockSpec((tt, COL_BLOCK), lambda b, t: (b * nt + t, 3)),
            pl.BlockSpec((tt, COL_BLOCK), lambda b, t: (b * nt + t, 4)),
            pl.BlockSpec((CONV_WIDTH, SUBLANES, D_CONV), lambda b, t: (0, 0, 0)),
            vec, vec, vec,
        ],
        out_specs=pl.BlockSpec((tt, D_CONV), lambda b, t: (b * nt + t, 0)),
        scratch_shapes=[pltpu.VMEM((CONV_HIST + tt, D_CONV), F32),
                        pltpu.VMEM((SUBLANES - 1, tt + CONV_SHIFT_ROWS, D_CONV), F32)],
        compiler_params=_params("parallel", "arbitrary"),
        name="conv_branch",
    )(p, p, rep(w_dw), rep(b_dw), rep(ln_g), rep(ln_b))


MERGE_COLS = 512


def _merge_kernel(a_ref, c_ref, ga0_ref, ga1_ref, gc0_ref, gc1_ref, x_ref, wa_ref, wb_ref, wo_ref, o_ref, m_ref):
    ga_refs = (ga0_ref, ga1_ref)
    gc_refs = (gc0_ref, gc1_ref)
    per_block = COL_BLOCK // MERGE_COLS
    for n in range(D_MODEL // MERGE_COLS):
        cs = slice(n * MERGE_COLS, (n + 1) * MERGE_COLS)
        gs = slice((n % per_block) * MERGE_COLS, (n % per_block + 1) * MERGE_COLS)
        ya = jnp.dot(a_ref[...], wa_ref[:, cs], preferred_element_type=F32)
        yc = jnp.dot(c_ref[...], wb_ref[:, cs], preferred_element_type=F32)
        ga = jax.nn.sigmoid(ga_refs[n // per_block][:, gs].astype(F32))
        gc = jax.nn.sigmoid(gc_refs[n // per_block][:, gs].astype(F32))
        m_ref[:, cs] = (ga * ya + gc * yc).astype(BF16)
    for n in range(D_MODEL // MERGE_COLS):
        cs = slice(n * MERGE_COLS, (n + 1) * MERGE_COLS)
        o_ref[:, cs] = x_ref[:, cs] + jnp.dot(m_ref[...], wo_ref[:, cs], preferred_element_type=F32)


def _merge(p, a, c, x, wa, wb, wo, layer):
    T = x.shape[0]
    tm = min(512, T)
    gate = lambda blk: pl.BlockSpec((tm, COL_BLOCK), lambda i: (i, blk))
    resident = lambda shape: pl.BlockSpec((None,) + shape, lambda i: (layer, 0, 0), pipeline_mode=pl.Buffered(1))
    return pl.pallas_call(
        _merge_kernel,
        out_shape=jax.ShapeDtypeStruct((T, D_MODEL), F32),
        grid=(T // tm,),
        in_specs=[
            pl.BlockSpec((tm, D_ATTN), lambda i: (i, 0)),
            pl.BlockSpec((tm, D_CONV), lambda i: (i, 0)),
            gate(5), gate(6), gate(7), gate(8),
            pl.BlockSpec((tm, D_MODEL), lambda i: (i, 0)),
            resident((D_ATTN, D_MODEL)),
            resident((D_CONV, D_MODEL)),
            resident((D_MODEL, D_MODEL)),
        ],
        out_specs=pl.BlockSpec((tm, D_MODEL), lambda i: (i, 0)),
        scratch_shapes=[pltpu.VMEM((tm, D_MODEL), BF16)],
        compiler_params=_params("parallel"),
        name="merge_out_proj",
    )(a, c, p, p, p, p, x, wa, wb, wo)


def _ffn_kernel(te_ref, ta_ref, x_ref, g_ref, wg_ref, wu_ref, wd_ref, o_ref, h_ref, *, residual):
    del te_ref
    i = pl.program_id(0)
    f = pl.program_id(1)

    @pl.when(f == 0)
    def _():
        _rms_rows_to(h_ref, x_ref, g_ref)
        if residual:
            o_ref[...] = x_ref[...]
        else:
            o_ref[...] = jnp.zeros(o_ref.shape, o_ref.dtype)

    @pl.when(ta_ref[i] == 1)
    def _():
        h = h_ref[...]
        gt = jnp.dot(h, wg_ref[...], preferred_element_type=F32)
        up = jnp.dot(h, wu_ref[...], preferred_element_type=F32)
        act = (gt * jax.nn.sigmoid(gt) * up).astype(BF16)
        o_ref[...] += jnp.dot(act, wd_ref[...], preferred_element_type=F32)


def _ffn(x, g, wg, wu, wd, layer, tile_expert, tile_active, *, residual, tm):
    R = x.shape[0]
    tf = 512
    nf = D_FF // tf
    fsel = lambda i, f, te, ta: jnp.where(ta[i] == 1, f, nf - 1)
    up_spec = pl.BlockSpec((None, None, D_MODEL, tf), lambda i, f, te, ta: (layer, te[i], 0, fsel(i, f, te, ta)))
    down_spec = pl.BlockSpec((None, None, tf, D_MODEL), lambda i, f, te, ta: (layer, te[i], fsel(i, f, te, ta), 0))
    grid_spec = pltpu.PrefetchScalarGridSpec(
        num_scalar_prefetch=2,
        grid=(R // tm, nf),
        in_specs=[
            pl.BlockSpec((tm, D_MODEL), lambda i, f, te, ta: (i, 0)),
            pl.BlockSpec((1, D_MODEL), lambda i, f, te, ta: (0, 0)),
            up_spec, up_spec, down_spec,
        ],
        out_specs=pl.BlockSpec((tm, D_MODEL), lambda i, f, te, ta: (i, 0)),
        scratch_shapes=[pltpu.VMEM((tm, D_MODEL), BF16)],
    )
    return pl.pallas_call(
        functools.partial(_ffn_kernel, residual=residual),
        out_shape=jax.ShapeDtypeStruct((R, D_MODEL), F32),
        grid_spec=grid_spec,
        compiler_params=_params("parallel", "arbitrary"),
        name="swiglu_residual" if residual else "swiglu_grouped",
    )(tile_expert, tile_active, x, g, wg, wu, wd)


ROUTER_ROWS = 128


def _router_kernel(x_ref, g_ref, wh_ref, wl_ref, meta_ref, gate_ref, cnt_ref, tri_ref, carry_ref, logit_ref):
    i = pl.program_id(0)
    tm = x_ref.shape[0]

    @pl.when(i == 0)
    def _():
        r = lax.broadcasted_iota(jnp.int32, (tm, tm), 0)
        c = lax.broadcasted_iota(jnp.int32, (tm, tm), 1)
        tri_ref[...] = jnp.where(c < r, 1.0, 0.0).astype(BF16)
        carry_ref[...] = jnp.zeros(carry_ref.shape, F32)

    def body(r, carry):
        sl = pl.ds(pl.multiple_of(r * ROUTER_ROWS, ROUTER_ROWS), ROUTER_ROWS)
        x = x_ref[sl, :]
        ms = jnp.mean(x * x, axis=-1, keepdims=True)
        h = x * lax.rsqrt(ms + EPS) * g_ref[...]
        hi = h.astype(BF16)
        lo = (h - hi.astype(F32)).astype(BF16)
        logit_ref[sl, :] = (jnp.dot(hi, wh_ref[...], preferred_element_type=F32)
                            + jnp.dot(lo, wh_ref[...], preferred_element_type=F32)
                            + jnp.dot(hi, wl_ref[...], preferred_element_type=F32))
        return carry

    lax.fori_loop(0, tm // ROUTER_ROWS, body, 0)

    lane = lax.broadcasted_iota(jnp.int32, (tm, LANES), 1).astype(F32)
    logits = jnp.where(lane < N_EXPERTS, logit_ref[...], -jnp.inf)
    m0 = jnp.max(logits, axis=-1, keepdims=True)
    e0 = jnp.min(jnp.where(logits == m0, lane, float(LANES)), axis=-1, keepdims=True)
    rest = jnp.where(lane == e0, -jnp.inf, logits)
    m1 = jnp.max(rest, axis=-1, keepdims=True)
    e1 = jnp.min(jnp.where(rest == m1, lane, float(LANES)), axis=-1, keepdims=True)
    t = jnp.exp(m1 - m0)
    w0 = 1.0 / (1.0 + t)
    w1 = t / (1.0 + t)

    onehot = jnp.where(jnp.logical_or(lane == e0, lane == e1), 1.0, 0.0)
    before = jnp.dot(tri_ref[...], onehot.astype(BF16), preferred_element_type=F32) + carry_ref[0:1, :]
    r0 = jnp.sum(jnp.where(lane == e0, before, 0.0), axis=-1, keepdims=True)
    r1 = jnp.sum(jnp.where(lane == e1, before, 0.0), axis=-1, keepdims=True)
    carry_ref[0:1, :] = carry_ref[0:1, :] + jnp.sum(onehot, axis=0, keepdims=True)

    meta = jnp.where(lane == 0, e0, jnp.where(lane == 1, e1, jnp.where(lane == 2, r0, jnp.where(lane == 3, r1, 0.0))))
    meta_ref[...] = meta.astype(jnp.int32)
    gate_ref[...] = jnp.where(lane == 0, w0, jnp.where(lane == 1, w1, 0.0))
    cnt_ref[...] = carry_ref[...]


def _router(x, g, w_router):
    T = x.shape[0]
    tm = min(1024, T)
    w_pad = jnp.zeros((D_MODEL, LANES), F32).at[:, :N_EXPERTS].set(w_router.astype(F32))
    wh = w_pad.astype(BF16)
    wl = (w_pad - wh.astype(F32)).astype(BF16)
    wspec = pl.BlockSpec((D_MODEL, LANES), lambda i: (0, 0))
    return pl.pallas_call(
        _router_kernel,
        out_shape=(jax.ShapeDtypeStruct((T, LANES), jnp.int32),
                   jax.ShapeDtypeStruct((T, LANES), F32),
                   jax.ShapeDtypeStruct((8, LANES), F32)),
        grid=(T // tm,),
        in_specs=[pl.BlockSpec((tm, D_MODEL), lambda i: (i, 0)), pl.BlockSpec((1, D_MODEL), lambda i: (0, 0)), wspec, wspec],
        out_specs=(pl.BlockSpec((tm, LANES), lambda i: (i, 0)),
                   pl.BlockSpec((tm, LANES), lambda i: (i, 0)),
                   pl.BlockSpec((8, LANES), lambda i: (0, 0))),
        scratch_shapes=[pltpu.VMEM((tm, tm), BF16), pltpu.VMEM((8, LANES), F32), pltpu.VMEM((tm, LANES), F32)],
        compiler_params=_params("arbitrary"),
        name="moe_router",
    )(x, g, wh, wl)


ROW_DMA_UNROLL = 8


def _scatter_rows_kernel(dest_ref, x_ref, init_hbm, o_hbm, sem, *, tc):
    del init_hbm

    def copy(t, k):
        d = dest_ref[0, 0, 2 * t + k]
        return pltpu.make_async_copy(x_ref.at[pl.ds(t, 1)], o_hbm.at[pl.ds(d, 1)], sem)

    def issue(t, carry):
        copy(t, 0).start()
        copy(t, 1).start()
        return carry

    def drain(t, carry):
        copy(t, 0).wait()
        copy(t, 1).wait()
        return carry

    lax.fori_loop(0, tc, issue, 0, unroll=ROW_DMA_UNROLL)
    lax.fori_loop(0, tc, drain, 0, unroll=ROW_DMA_UNROLL)


def _scatter_rows(x, dest, n_rows):
    T = x.shape[0]
    tc = min(256, T)
    dest3 = dest.reshape(T // tc, 1, 2 * tc)
    init = jnp.zeros((n_rows, D_MODEL), x.dtype)
    return pl.pallas_call(
        functools.partial(_scatter_rows_kernel, tc=tc),
        out_shape=jax.ShapeDtypeStruct((n_rows, D_MODEL), x.dtype),
        grid=(T // tc,),
        in_specs=[
            pl.BlockSpec((1, 1, 2 * tc), lambda i: (i, 0, 0), memory_space=pltpu.SMEM),
            pl.BlockSpec((tc, D_MODEL), lambda i: (i, 0)),
            pl.BlockSpec(memory_space=pl.ANY),
        ],
        out_specs=pl.BlockSpec(memory_space=pl.ANY),
        scratch_shapes=[pltpu.SemaphoreType.DMA(())],
        input_output_aliases={2: 0},
        compiler_params=_params("arbitrary"),
        name="moe_scatter_rows",
    )(dest3, x, init)


def _combine_kernel(dest_ref, next_ref, x_ref, gate_ref, y_hbm, o_ref, buf_ref, sem, *, tc):
    i = pl.program_id(0)
    slot = lax.rem(i, 2)

    def copy(idx_ref, s, t, k):
        d = idx_ref[0, 0, 2 * t + k]
        return pltpu.make_async_copy(y_hbm.at[pl.ds(d, 1)], buf_ref.at[s, k, pl.ds(t, 1)], sem.at[s])

    def start_tile(idx_ref, s):
        def body(t, carry):
            copy(idx_ref, s, t, 0).start()
            copy(idx_ref, s, t, 1).start()
            return carry

        lax.fori_loop(0, tc, body, 0, unroll=ROW_DMA_UNROLL)

    @pl.when(i == 0)
    def _():
        start_tile(dest_ref, slot)

    @pl.when(i + 1 < pl.num_programs(0))
    def _():
        start_tile(next_ref, 1 - slot)

    def drain(t, carry):
        copy(dest_ref, slot, t, 0).wait()
        copy(dest_ref, slot, t, 1).wait()
        return carry

    lax.fori_loop(0, tc, drain, 0, unroll=ROW_DMA_UNROLL)
    w0 = gate_ref[:, 0:1]
    w1 = gate_ref[:, 1:2]
    o_ref[...] = x_ref[...] + w0 * buf_ref[slot, 0] + w1 * buf_ref[slot, 1]


def _combine(x, gate, y, dest):
    T = x.shape[0]
    tc = min(256, T)
    n = T // tc
    dest3 = dest.reshape(n, 1, 2 * tc)
    return pl.pallas_call(
        functools.partial(_combine_kernel, tc=tc),
        out_shape=jax.ShapeDtypeStruct((T, D_MODEL), F32),
        grid=(n,),
        in_specs=[
            pl.BlockSpec((1, 1, 2 * tc), lambda i: (i, 0, 0), memory_space=pltpu.SMEM),
            pl.BlockSpec((1, 1, 2 * tc), lambda i: (jnp.minimum(i + 1, n - 1), 0, 0), memory_space=pltpu.SMEM),
            pl.BlockSpec((tc, D_MODEL), lambda i: (i, 0)),
            pl.BlockSpec((tc, LANES), lambda i: (i, 0)),
            pl.BlockSpec(memory_space=pl.ANY),
        ],
        out_specs=pl.BlockSpec((tc, D_MODEL), lambda i: (i, 0)),
        scratch_shapes=[pltpu.VMEM((2, 2, tc, D_MODEL), F32), pltpu.SemaphoreType.DMA((2,))],
        compiler_params=_params("arbitrary"),
        name="moe_combine",
    )(dest3, dest3, x, gate, y)


def _moe(x, g, w_router, wg, wu, wd, layer):
    T = x.shape[0]
    tm = min(1024, T)
    meta, gate, cnt = _router(x, g, w_router)
    expert = meta[:, 0:2]
    rank = meta[:, 2:4]
    counts = cnt[0, :N_EXPERTS].astype(jnp.int32)
    tiles_per = (counts + tm - 1) // tm
    tile_end = jnp.cumsum(tiles_per)
    offset = (tile_end - tiles_per) * tm
    dest = offset[expert] + rank
    n_tiles = (2 * T) // tm + N_EXPERTS
    tile_id = jnp.arange(n_tiles)
    tile_active = (tile_id < tile_end[-1]).astype(jnp.int32)
    tile_expert = jnp.searchsorted(tile_end, jnp.minimum(tile_id, tile_end[-1] - 1), side="right").astype(jnp.int32)
    xs = _scatter_rows(x, dest, n_tiles * tm)
    ys = _ffn(xs, g, wg, wu, wd, layer, tile_expert, tile_active, residual=False, tm=tm)
    return _combine(x, gate, ys, dest)


def kernel(x, mix_norm_g, w_in, q_norm_g, k_norm_g, rel_bias, conv_dw_w, conv_dw_b, conv_ln_g, conv_ln_b,
           w_branch_attn, w_branch_conv, w_out, ffn_norm_g, dense_w_gate, dense_w_up, dense_w_down,
           moe_router, moe_w_gate, moe_w_up, moe_w_down):
    B, S, D = x.shape
    T = B * S
    depth = w_in.shape[0]
    qrows = 256
    conv_tile = 256
    row = lambda v: v.astype(F32)[None]
    bias_tab = _bias_table(rel_bias, qrows)
    xf = x.reshape(T, D).astype(F32)
    dense_tiles = jnp.zeros((T // min(1024, T),), jnp.int32)
    cast = _cast_bf16
    w_in_b, wa_b, wb_b, wo_b = cast(w_in), cast(w_branch_attn), cast(w_branch_conv), cast(w_out)
    dense_b = [cast(w)[:, None] for w in (dense_w_gate, dense_w_up, dense_w_down)]
    moe_b = [cast(w) for w in (moe_w_gate, moe_w_up, moe_w_down)]
    for l in range(depth):
        p = _in_proj(xf, row(mix_norm_g[l]), w_in_b, l)
        a = _attention(p, bias_tab, q_norm_g[l], k_norm_g[l], B, S, qrows)
        c = _conv_branch(p, conv_dw_w[l], conv_dw_b[l], conv_ln_g[l], conv_ln_b[l], B, S, conv_tile)
        xf = _merge(p, a, c, xf, wa_b, wb_b, wo_b, l)
        i = l // 2
        if l % 2 == 0:
            xf = _ffn(xf, row(ffn_norm_g[l]), *dense_b, i, dense_tiles, dense_tiles + 1,
                      residual=True, tm=min(1024, T))
        else:
            xf = _moe(xf, row(ffn_norm_g[l]), moe_router[i], *moe_b, i)
    return xf.reshape(B, S, D).astype(x.dtype)
```

```python
import functools

import jax
import jax.numpy as jnp
from jax import lax
from jax.experimental import pallas as pl
from jax.experimental.pallas import tpu as pltpu

F32 = jnp.float32
BF16 = jnp.bfloat16

D_MODEL = 2048
CHUNK = 64
LEFT = 8 * CHUNK
N_HEADS = 16
HEAD_DIM = 64
D_ATTN = N_HEADS * HEAD_DIM
MAX_REL = 256
D_CONV = 1024
CONV_WIDTH = 31
D_FF = 7168
N_EXPERTS = 8
EPS = 1e-6
D_IN = 9216
COL_BLOCK = 1024

LANES = 128
V7X_VMEM_BYTES = 64 * 1024 * 1024
VMEM_LIMIT = V7X_VMEM_BYTES - 8 * 1024 * 1024

NEG_INF = -1e30
LOG2_E = 1.4426950408889634


def _params(*sem):
    return pltpu.CompilerParams(dimension_semantics=sem, vmem_limit_bytes=VMEM_LIMIT)


def _rms_rows_to(dst_ref, x_ref, g_ref, rows=64):
    def body(r, carry):
        sl = pl.ds(pl.multiple_of(r * rows, rows), rows)
        x = x_ref[sl, :]
        ms = jnp.mean(x * x, axis=-1, keepdims=True)
        dst_ref[sl, :] = (x * lax.rsqrt(ms + EPS) * g_ref[...]).astype(dst_ref.dtype)
        return carry

    lax.fori_loop(0, x_ref.shape[0] // rows, body, 0, unroll=4)


CAST_BLOCK_BYTES = 8 * 1024 * 1024


def _cast_kernel(x_ref, o_ref):
    o_ref[...] = x_ref[...].astype(o_ref.dtype)


def _cast_bf16(w):
    C = w.shape[-1]
    flat = w.reshape(-1, C)
    rows = min(flat.shape[0], pl.next_power_of_2(CAST_BLOCK_BYTES // (C * 4) + 1) // 2)
    assert flat.shape[0] % rows == 0 and rows % 16 == 0
    out = pl.pallas_call(
        _cast_kernel,
        out_shape=jax.ShapeDtypeStruct(flat.shape, BF16),
        grid=(flat.shape[0] // rows,),
        in_specs=[pl.BlockSpec((rows, C), lambda i: (i, 0))],
        out_specs=pl.BlockSpec((rows, C), lambda i: (i, 0)),
        compiler_params=_params("parallel"),
        name="cast_bf16",
    )(flat)
    return out.reshape(w.shape)


def _in_proj_kernel(x_ref, g_ref, w_ref, o_ref, h_ref):
    @pl.when(pl.program_id(1) == 0)
    def _():
        _rms_rows_to(h_ref, x_ref, g_ref)

    o_ref[...] = jnp.dot(h_ref[...], w_ref[...], preferred_element_type=F32).astype(o_ref.dtype)


def _in_proj(x, g, w, layer):
    T = x.shape[0]
    tm = min(1024, T)
    tn = D_IN // 4
    return pl.pallas_call(
        _in_proj_kernel,
        out_shape=jax.ShapeDtypeStruct((T, D_IN), BF16),
        grid=(T // tm, D_IN // tn),
        in_specs=[
            pl.BlockSpec((tm, D_MODEL), lambda i, j: (i, 0)),
            pl.BlockSpec((1, D_MODEL), lambda i, j: (0, 0)),
            pl.BlockSpec((None, D_MODEL, tn), lambda i, j: (layer, 0, j)),
        ],
        out_specs=pl.BlockSpec((tm, tn), lambda i, j: (i, j)),
        scratch_shapes=[pltpu.VMEM((tm, D_MODEL), BF16)],
        compiler_params=_params("parallel", "arbitrary"),
        name="in_proj",
    )(x, g, w)


def _pair_rms_scale(xp, ones_ref):
    ss = jnp.dot((xp * xp).astype(BF16), ones_ref[...], preferred_element_type=F32)
    return lax.rsqrt(ss * (1.0 / HEAD_DIM) + EPS)


def _attn_kernel(q_ref, k_ref, v_ref, bias_ref, gq_ref, gk_ref, ones_ref, o_ref, kn_ref, qn_ref, *, qrows, win):
    qb = pl.program_id(1)
    n_pairs = D_ATTN // LANES

    @pl.when(qb == 0)
    def _():
        rows = 128

        def body(r, carry):
            sl = pl.ds(pl.multiple_of(r * rows, rows), rows)
            for hp in range(n_pairs):
                cs = slice(hp * LANES, (hp + 1) * LANES)
                kp = k_ref[sl, cs].astype(F32)
                kn_ref[sl, cs] = (kp * _pair_rms_scale(kp, ones_ref) * gk_ref[...]).astype(BF16)
            return carry

        lax.fori_loop(0, k_ref.shape[0] // rows, body, 0)

    lane = lax.broadcasted_iota(jnp.int32, (qrows, LANES), 1)
    lo = lane < HEAD_DIM
    for hp in range(n_pairs):
        cs = slice(hp * LANES, (hp + 1) * LANES)
        qp = q_ref[:, cs].astype(F32)
        qn = qp * _pair_rms_scale(qp, ones_ref) * gq_ref[...]
        qn_ref[hp, 0:qrows, :] = jnp.where(lo, qn, 0.0).astype(BF16)
        qn_ref[hp, qrows:2 * qrows, :] = jnp.where(lo, 0.0, qn).astype(BF16)

    ws = pl.multiple_of(jnp.maximum(qb * qrows - LEFT, 0), CHUNK)
    for hp in range(n_pairs):
        cs = slice(hp * LANES, (hp + 1) * LANES)
        kw = kn_ref[pl.ds(ws, win), cs]
        vw = v_ref[pl.ds(ws, win), cs]
        s = lax.dot_general(qn_ref[hp], kw, (((1,), (1,)), ((), ())), preferred_element_type=F32)
        s = s + bias_ref[2 * hp:2 * hp + 2].reshape(2 * qrows, win)
        m = jnp.max(s, axis=-1, keepdims=True)
        e = jnp.exp2(s - m)
        vx = jnp.concatenate([vw, jnp.ones_like(vw)], axis=1)
        ox = jnp.dot(e.astype(BF16), vx, preferred_element_type=F32)
        o = ox[:, 0:LANES] / ox[:, LANES:2 * LANES]
        o_ref[:, cs] = jnp.where(lo, o[0:qrows], o[qrows:2 * qrows]).astype(o_ref.dtype)


def _bias_table(rel_bias, qrows):
    win = LEFT + qrows
    n_var = LEFT // qrows + 1
    period = win + qrows
    diag = jnp.concatenate([jnp.arange(win), jnp.arange(-qrows, 0)])
    off = (jnp.arange(n_var) * qrows)[:, None]
    rel = jnp.clip(off - diag[None, :], -MAX_REL, MAX_REL) + MAX_REL
    vec = jnp.transpose(rel_bias.astype(F32)[:, rel], (1, 0, 2))
    flat = jnp.tile(vec, (1, 1, qrows))[..., :qrows * (period - 1)]
    bias = flat.reshape(n_var, N_HEADS, qrows, period - 1)[..., :win]
    qi = off[:, :, None] + jnp.arange(qrows)[None, :, None]
    kj = jnp.arange(win)[None, None, :]
    dchunk = qi // CHUNK - kj // CHUNK
    valid = (dchunk >= 0) & (dchunk <= LEFT // CHUNK)
    return jnp.where(valid[:, None], bias * LOG2_E, NEG_INF)


def _attention(p, bias_tab, gq, gk, B, S, qrows):
    T = B * S
    win = LEFT + qrows
    nq = S // qrows
    n_var = bias_tab.shape[0]
    scale = HEAD_DIM ** -0.5
    gq2 = (jnp.tile(gq.astype(F32), 2) * (scale * LOG2_E))[None]
    gk2 = jnp.tile(gk.astype(F32), 2)[None]
    head_id = jnp.arange(LANES) // HEAD_DIM
    head_ones = (head_id[:, None] == head_id[None, :]).astype(BF16)
    kern = functools.partial(_attn_kernel, qrows=qrows, win=win)
    return pl.pallas_call(
        kern,
        out_shape=jax.ShapeDtypeStruct((T, D_ATTN), BF16),
        grid=(B, nq),
        in_specs=[
            pl.BlockSpec((qrows, COL_BLOCK), lambda b, q: (b * nq + q, 0)),
            pl.BlockSpec((S, COL_BLOCK), lambda b, q: (b, 1), pipeline_mode=pl.Buffered(1)),
            pl.BlockSpec((S, COL_BLOCK), lambda b, q: (b, 2)),
            pl.BlockSpec((None, N_HEADS, qrows, win), lambda b, q: (jnp.minimum(q, n_var - 1), 0, 0, 0)),
            pl.BlockSpec((1, LANES), lambda b, q: (0, 0)),
            pl.BlockSpec((1, LANES), lambda b, q: (0, 0)),
            pl.BlockSpec((LANES, LANES), lambda b, q: (0, 0)),
        ],
        out_specs=pl.BlockSpec((qrows, D_ATTN), lambda b, q: (b * nq + q, 0)),
        scratch_shapes=[pltpu.VMEM((S, D_ATTN), BF16), pltpu.VMEM((D_ATTN // LANES, 2 * qrows, LANES), BF16)],
        compiler_params=_params("parallel", "arbitrary"),
        name="band_attention",
    )(p, p, p, bias_tab, gq2, gk2, head_ones)


CONV_HIST = 32
SUBLANES = 8
CONV_ROWS = 2 * SUBLANES
CONV_SHIFT_ROWS = CONV_HIST - SUBLANES


def _conv_kernel(a_ref, g_ref, w_ref, b_ref, lg_ref, lb_ref, o_ref, y_ref, ys_ref, *, tt):
    t = pl.program_id(1)

    @pl.when(t == 0)
    def _():
        y_ref[0:CONV_HIST, :] = jnp.zeros((CONV_HIST, D_CONV), F32)

    @pl.when(t > 0)
    def _():
        y_ref[0:CONV_HIST, :] = y_ref[tt:tt + CONV_HIST, :]

    a = a_ref[...].astype(F32)
    g = g_ref[...].astype(F32)
    y_ref[CONV_HIST:CONV_HIST + tt, :] = a * jax.nn.sigmoid(g)

    n_shift = tt + CONV_SHIFT_ROWS
    for r in range(1, SUBLANES):
        ys_ref[r - 1] = y_ref[r:r + n_shift, :]

    shift = CONV_HIST - (CONV_WIDTH - 1)
    halves = CONV_ROWS // SUBLANES
    for rb in range(tt // CONV_ROWS):
        r0 = rb * CONV_ROWS
        acc = [b_ref[...] for _ in range(halves)]
        for j in range(CONV_WIDTH):
            a8, r = divmod(shift + j, SUBLANES)
            src = y_ref if r == 0 else ys_ref.at[r - 1]
            wj = w_ref[j]
            for hf in range(halves):
                m0 = r0 + (a8 + hf) * SUBLANES
                acc[hf] = acc[hf] + wj * src[m0:m0 + SUBLANES, :]
        for hf in range(halves):
            mu = jnp.mean(acc[hf], axis=-1, keepdims=True)
            xc = acc[hf] - mu
            var = jnp.mean(xc * xc, axis=-1, keepdims=True)
            z = xc * lax.rsqrt(var + EPS) * lg_ref[...] + lb_ref[...]
            rows = slice(r0 + hf * SUBLANES, r0 + (hf + 1) * SUBLANES)
            o_ref[rows, :] = (z * jax.nn.sigmoid(z)).astype(o_ref.dtype)


def _conv_branch(p, w_dw, b_dw, ln_g, ln_b, B, S, tt):
    T = B * S
    nt = S // tt
    rep = lambda v: jnp.broadcast_to(v.astype(F32)[..., None, :], v.shape[:-1] + (SUBLANES, D_CONV))
    vec = pl.BlockSpec((SUBLANES, D_CONV), lambda b, t: (0, 0))
    return pl.pallas_call(
        functools.partial(_conv_kernel, tt=tt),
        out_shape=jax.ShapeDtypeStruct((T, D_CONV), BF16),
        grid=(B, nt),
        in_specs=[
            pl.BlockSpec((tt, COL_BLOCK), lambda b, t: (b * nt + t, 3)),
            pl.BlockSpec((tt, COL_BLOCK), lambda b, t: (b * nt + t, 4)),
            pl.BlockSpec((CONV_WIDTH, SUBLANES, D_CONV), lambda b, t: (0, 0, 0)),
            vec, vec, vec,
        ],
        out_specs=pl.BlockSpec((tt, D_CONV), lambda b, t: (b * nt + t, 0)),
        scratch_shapes=[pltpu.VMEM((CONV_HIST + tt, D_CONV), F32),
                        pltpu.VMEM((SUBLANES - 1, tt + CONV_SHIFT_ROWS, D_CONV), F32)],
        compiler_params=_params("parallel", "arbitrary"),
        name="conv_branch",
    )(p, p, rep(w_dw), rep(b_dw), rep(ln_g), rep(ln_b))


MERGE_COLS = 512


def _merge_kernel(a_ref, c_ref, ga0_ref, ga1_ref, gc0_ref, gc1_ref, x_ref, wa_ref, wb_ref, wo_ref, o_ref, m_ref):
    ga_refs = (ga0_ref, ga1_ref)
    gc_refs = (gc0_ref, gc1_ref)
    per_block = COL_BLOCK // MERGE_COLS
    for n in range(D_MODEL // MERGE_COLS):
        cs = slice(n * MERGE_COLS, (n + 1) * MERGE_COLS)
        gs = slice((n % per_block) * MERGE_COLS, (n % per_block + 1) * MERGE_COLS)
        ya = jnp.dot(a_ref[...], wa_ref[:, cs], preferred_element_type=F32)
        yc = jnp.dot(c_ref[...], wb_ref[:, cs], preferred_element_type=F32)
        ga = jax.nn.sigmoid(ga_refs[n // per_block][:, gs].astype(F32))
        gc = jax.nn.sigmoid(gc_refs[n // per_block][:, gs].astype(F32))
        m_ref[:, cs] = (ga * ya + gc * yc).astype(BF16)
    for n in range(D_MODEL // MERGE_COLS):
        cs = slice(n * MERGE_COLS, (n + 1) * MERGE_COLS)
        o_ref[:, cs] = x_ref[:, cs] + jnp.dot(m_ref[...], wo_ref[:, cs], preferred_element_type=F32)


def _merge(p, a, c, x, wa, wb, wo, layer):
    T = x.shape[0]
    tm = min(512, T)
    gate = lambda blk: pl.BlockSpec((tm, COL_BLOCK), lambda i: (i, blk))
    resident = lambda shape: pl.BlockSpec((None,) + shape, lambda i: (layer, 0, 0), pipeline_mode=pl.Buffered(1))
    return pl.pallas_call(
        _merge_kernel,
        out_shape=jax.ShapeDtypeStruct((T, D_MODEL), F32),
        grid=(T // tm,),
        in_specs=[
            pl.BlockSpec((tm, D_ATTN), lambda i: (i, 0)),
            pl.BlockSpec((tm, D_CONV), lambda i: (i, 0)),
            gate(5), gate(6), gate(7), gate(8),
            pl.BlockSpec((tm, D_MODEL), lambda i: (i, 0)),
            resident((D_ATTN, D_MODEL)),
            resident((D_CONV, D_MODEL)),
            resident((D_MODEL, D_MODEL)),
        ],
        out_specs=pl.BlockSpec((tm, D_MODEL), lambda i: (i, 0)),
        scratch_shapes=[pltpu.VMEM((tm, D_MODEL), BF16)],
        compiler_params=_params("parallel"),
        name="merge_out_proj",
    )(a, c, p, p, p, p, x, wa, wb, wo)


def _ffn_kernel(te_ref, ta_ref, x_ref, g_ref, wg_ref, wu_ref, wd_ref, o_ref, h_ref, *, residual):
    del te_ref
    i = pl.program_id(0)
    f = pl.program_id(1)

    @pl.when(f == 0)
    def _():
        _rms_rows_to(h_ref, x_ref, g_ref)
        if residual:
            o_ref[...] = x_ref[...]
        else:
            o_ref[...] = jnp.zeros(o_ref.shape, o_ref.dtype)

    @pl.when(ta_ref[i] == 1)
    def _():
        h = h_ref[...]
        gt = jnp.dot(h, wg_ref[...], preferred_element_type=F32)
        up = jnp.dot(h, wu_ref[...], preferred_element_type=F32)
        act = (gt * jax.nn.sigmoid(gt) * up).astype(BF16)
        o_ref[...] += jnp.dot(act, wd_ref[...], preferred_element_type=F32)


def _ffn(x, g, wg, wu, wd, layer, tile_expert, tile_active, *, residual, tm, tf):
    R = x.shape[0]
    nf = D_FF // tf
    fsel = lambda i, f, te, ta: jnp.where(ta[i] == 1, f, nf - 1)
    up_spec = pl.BlockSpec((None, None, D_MODEL, tf), lambda i, f, te, ta: (layer, te[i], 0, fsel(i, f, te, ta)))
    down_spec = pl.BlockSpec((None, None, tf, D_MODEL), lambda i, f, te, ta: (layer, te[i], fsel(i, f, te, ta), 0))
    grid_spec = pltpu.PrefetchScalarGridSpec(
        num_scalar_prefetch=2,
        grid=(R // tm, nf),
        in_specs=[
            pl.BlockSpec((tm, D_MODEL), lambda i, f, te, ta: (i, 0)),
            pl.BlockSpec((1, D_MODEL), lambda i, f, te, ta: (0, 0)),
            up_spec, up_spec, down_spec,
        ],
        out_specs=pl.BlockSpec((tm, D_MODEL), lambda i, f, te, ta: (i, 0)),
        scratch_shapes=[pltpu.VMEM((tm, D_MODEL), BF16)],
    )
    return pl.pallas_call(
        functools.partial(_ffn_kernel, residual=residual),
        out_shape=jax.ShapeDtypeStruct((R, D_MODEL), F32),
        grid_spec=grid_spec,
        compiler_params=_params("parallel", "arbitrary"),
        name="swiglu_residual" if residual else "swiglu_grouped",
    )(tile_expert, tile_active, x, g, wg, wu, wd)


ROUTER_ROWS = 128


def _router_kernel(x_ref, g_ref, wh_ref, wl_ref, meta_ref, gate_ref, cnt_ref, tri_ref, carry_ref, logit_ref):
    i = pl.program_id(0)
    tm = x_ref.shape[0]

    @pl.when(i == 0)
    def _():
        r = lax.broadcasted_iota(jnp.int32, (tm, tm), 0)
        c = lax.broadcasted_iota(jnp.int32, (tm, tm), 1)
        tri_ref[...] = jnp.where(c < r, 1.0, 0.0).astype(BF16)
        carry_ref[...] = jnp.zeros(carry_ref.shape, F32)

    def body(r, carry):
        sl = pl.ds(pl.multiple_of(r * ROUTER_ROWS, ROUTER_ROWS), ROUTER_ROWS)
        x = x_ref[sl, :]
        ms = jnp.mean(x * x, axis=-1, keepdims=True)
        h = x * lax.rsqrt(ms + EPS) * g_ref[...]
        hi = h.astype(BF16)
        lo = (h - hi.astype(F32)).astype(BF16)
        logit_ref[sl, :] = (jnp.dot(hi, wh_ref[...], preferred_element_type=F32)
                            + jnp.dot(lo, wh_ref[...], preferred_element_type=F32)
                            + jnp.dot(hi, wl_ref[...], preferred_element_type=F32))
        return carry

    lax.fori_loop(0, tm // ROUTER_ROWS, body, 0)

    lane = lax.broadcasted_iota(jnp.int32, (tm, LANES), 1).astype(F32)
    logits = jnp.where(lane < N_EXPERTS, logit_ref[...], -jnp.inf)
    m0 = jnp.max(logits, axis=-1, keepdims=True)
    e0 = jnp.min(jnp.where(logits == m0, lane, float(LANES)), axis=-1, keepdims=True)
    rest = jnp.where(lane == e0, -jnp.inf, logits)
    m1 = jnp.max(rest, axis=-1, keepdims=True)
    e1 = jnp.min(jnp.where(rest == m1, lane, float(LANES)), axis=-1, keepdims=True)
    t = jnp.exp(m1 - m0)
    w0 = 1.0 / (1.0 + t)
    w1 = t / (1.0 + t)

    onehot = jnp.where(jnp.logical_or(lane == e0, lane == e1), 1.0, 0.0)
    before = jnp.dot(tri_ref[...], onehot.astype(BF16), preferred_element_type=F32) + carry_ref[0:1, :]
    r0 = jnp.sum(jnp.where(lane == e0, before, 0.0), axis=-1, keepdims=True)
    r1 = jnp.sum(jnp.where(lane == e1, before, 0.0), axis=-1, keepdims=True)
    carry_ref[0:1, :] = carry_ref[0:1, :] + jnp.sum(onehot, axis=0, keepdims=True)

    meta = jnp.where(lane == 0, e0, jnp.where(lane == 1, e1, jnp.where(lane == 2, r0, jnp.where(lane == 3, r1, 0.0))))
    meta_ref[...] = meta.astype(jnp.int32)
    gate_ref[...] = jnp.where(lane == 0, w0, jnp.where(lane == 1, w1, 0.0))
    cnt_ref[...] = carry_ref[...]


def _router(x, g, w_router):
    T = x.shape[0]
    tm = min(1024, T)
    w_pad = jnp.zeros((D_MODEL, LANES), F32).at[:, :N_EXPERTS].set(w_router.astype(F32))
    wh = w_pad.astype(BF16)
    wl = (w_pad - wh.astype(F32)).astype(BF16)
    wspec = pl.BlockSpec((D_MODEL, LANES), lambda i: (0, 0))
    return pl.pallas_call(
        _router_kernel,
        out_shape=(jax.ShapeDtypeStruct((T, LANES), jnp.int32),
                   jax.ShapeDtypeStruct((T, LANES), F32),
                   jax.ShapeDtypeStruct((8, LANES), F32)),
        grid=(T // tm,),
        in_specs=[pl.BlockSpec((tm, D_MODEL), lambda i: (i, 0)), pl.BlockSpec((1, D_MODEL), lambda i: (0, 0)), wspec, wspec],
        out_specs=(pl.BlockSpec((tm, LANES), lambda i: (i, 0)),
                   pl.BlockSpec((tm, LANES), lambda i: (i, 0)),
                   pl.BlockSpec((8, LANES), lambda i: (0, 0))),
        scratch_shapes=[pltpu.VMEM((tm, tm), BF16), pltpu.VMEM((8, LANES), F32), pltpu.VMEM((tm, LANES), F32)],
        compiler_params=_params("arbitrary"),
        name="moe_router",
    )(x, g, wh, wl)


ROW_DMA_UNROLL = 8


def _scatter_rows_kernel(dest_ref, pad_ref, x_ref, o_hbm, zero_ref, sem, *, tc, pc):
    zero_ref[...] = jnp.zeros(zero_ref.shape, zero_ref.dtype)

    def copy(t, k):
        d = dest_ref[0, 0, 2 * t + k]
        return pltpu.make_async_copy(x_ref.at[pl.ds(t, 1)], o_hbm.at[pl.ds(d, 1)], sem)

    def fill(j):
        d = pad_ref[0, 0, j]
        return pltpu.make_async_copy(zero_ref.at[pl.ds(0, 1)], o_hbm.at[pl.ds(d, 1)], sem)

    def issue(t, carry):
        copy(t, 0).start()
        copy(t, 1).start()
        return carry

    def drain(t, carry):
        copy(t, 0).wait()
        copy(t, 1).wait()
        return carry

    def issue_fill(j, carry):
        fill(j).start()
        return carry

    def drain_fill(j, carry):
        fill(j).wait()
        return carry

    lax.fori_loop(0, tc, issue, 0, unroll=ROW_DMA_UNROLL)
    lax.fori_loop(0, pc, issue_fill, 0, unroll=ROW_DMA_UNROLL)
    lax.fori_loop(0, tc, drain, 0, unroll=ROW_DMA_UNROLL)
    lax.fori_loop(0, pc, drain_fill, 0, unroll=ROW_DMA_UNROLL)


def _scatter_rows(x, dest, pad_rows, n_rows):
    T = x.shape[0]
    tc = min(256, T)
    n = T // tc
    assert n_rows == 2 * T + pad_rows.shape[0] and pad_rows.shape[0] % n == 0
    pc = pad_rows.shape[0] // n
    return pl.pallas_call(
        functools.partial(_scatter_rows_kernel, tc=tc, pc=pc),
        out_shape=jax.ShapeDtypeStruct((n_rows, D_MODEL), x.dtype),
        grid=(n,),
        in_specs=[
            pl.BlockSpec((1, 1, 2 * tc), lambda i: (i, 0, 0), memory_space=pltpu.SMEM),
            pl.BlockSpec((1, 1, pc), lambda i: (i, 0, 0), memory_space=pltpu.SMEM),
            pl.BlockSpec((tc, D_MODEL), lambda i: (i, 0)),
        ],
        out_specs=pl.BlockSpec(memory_space=pl.ANY),
        scratch_shapes=[pltpu.VMEM((SUBLANES, D_MODEL), x.dtype), pltpu.SemaphoreType.DMA(())],
        compiler_params=_params("arbitrary"),
        name="moe_scatter_rows",
    )(dest.reshape(n, 1, 2 * tc), pad_rows.reshape(n, 1, pc), x)


def _combine_kernel(dest_ref, next_ref, x_ref, gate_ref, y_hbm, o_ref, buf_ref, sem, *, tc):
    i = pl.program_id(0)
    slot = lax.rem(i, 2)

    def copy(idx_ref, s, t, k):
        d = idx_ref[0, 0, 2 * t + k]
        return pltpu.make_async_copy(y_hbm.at[pl.ds(d, 1)], buf_ref.at[s, k, pl.ds(t, 1)], sem.at[s])

    def start_tile(idx_ref, s):
        def body(t, carry):
            copy(idx_ref, s, t, 0).start()
            copy(idx_ref, s, t, 1).start()
            return carry

        lax.fori_loop(0, tc, body, 0, unroll=ROW_DMA_UNROLL)

    @pl.when(i == 0)
    def _():
        start_tile(dest_ref, slot)

    @pl.when(i + 1 < pl.num_programs(0))
    def _():
        start_tile(next_ref, 1 - slot)

    def drain(t, carry):
        copy(dest_ref, slot, t, 0).wait()
        copy(dest_ref, slot, t, 1).wait()
        return carry

    lax.fori_loop(0, tc, drain, 0, unroll=ROW_DMA_UNROLL)
    w0 = gate_ref[:, 0:1]
    w1 = gate_ref[:, 1:2]
    o_ref[...] = x_ref[...] + w0 * buf_ref[slot, 0] + w1 * buf_ref[slot, 1]


def _combine(x, gate, y, dest):
    T = x.shape[0]
    tc = min(256, T)
    n = T // tc
    dest3 = dest.reshape(n, 1, 2 * tc)
    return pl.pallas_call(
        functools.partial(_combine_kernel, tc=tc),
        out_shape=jax.ShapeDtypeStruct((T, D_MODEL), F32),
        grid=(n,),
        in_specs=[
            pl.BlockSpec((1, 1, 2 * tc), lambda i: (i, 0, 0), memory_space=pltpu.SMEM),
            pl.BlockSpec((1, 1, 2 * tc), lambda i: (jnp.minimum(i + 1, n - 1), 0, 0), memory_space=pltpu.SMEM),
            pl.BlockSpec((tc, D_MODEL), lambda i: (i, 0)),
            pl.BlockSpec((tc, LANES), lambda i: (i, 0)),
            pl.BlockSpec(memory_space=pl.ANY),
        ],
        out_specs=pl.BlockSpec((tc, D_MODEL), lambda i: (i, 0)),
        scratch_shapes=[pltpu.VMEM((2, 2, tc, D_MODEL), F32), pltpu.SemaphoreType.DMA((2,))],
        compiler_params=_params("arbitrary"),
        name="moe_combine",
    )(dest3, dest3, x, gate, y)


def _moe(x, g, w_router, wg, wu, wd, layer):
    T = x.shape[0]
    tm = min(512, T)
    meta, gate, cnt = _router(x, g, w_router)
    expert = meta[:, 0:2]
    rank = meta[:, 2:4]
    counts = cnt[0, :N_EXPERTS].astype(jnp.int32)
    tiles_per = (counts + tm - 1) // tm
    tile_end = jnp.cumsum(tiles_per)
    offset = (tile_end - tiles_per) * tm
    dest = offset[expert] + rank
    n_tiles = (2 * T) // tm + N_EXPERTS
    tile_id = jnp.arange(n_tiles)
    tile_active = (tile_id < tile_end[-1]).astype(jnp.int32)
    tile_expert = jnp.searchsorted(tile_end, jnp.minimum(tile_id, tile_end[-1] - 1), side="right").astype(jnp.int32)
    n_rows = n_tiles * tm
    seg_start = jnp.concatenate([offset + counts, tile_end[-1:] * tm])
    seg_len = jnp.concatenate([tiles_per * tm - counts, n_rows - tile_end[-1:] * tm])
    seg_end = jnp.cumsum(seg_len)
    j = jnp.arange(n_rows - 2 * T)
    seg = jnp.searchsorted(seg_end, j, side="right")
    pad_rows = (seg_start[seg] + j - (seg_end - seg_len)[seg]).astype(jnp.int32)
    xs = _scatter_rows(x, dest, pad_rows, n_rows)
    ys = _ffn(xs, g, wg, wu, wd, layer, tile_expert, tile_active, residual=False, tm=tm, tf=1024)
    return _combine(x, gate, ys, dest)


def kernel(x, mix_norm_g, w_in, q_norm_g, k_norm_g, rel_bias, conv_dw_w, conv_dw_b, conv_ln_g, conv_ln_b,
           w_branch_attn, w_branch_conv, w_out, ffn_norm_g, dense_w_gate, dense_w_up, dense_w_down,
           moe_router, moe_w_gate, moe_w_up, moe_w_down):
    B, S, D = x.shape
    T = B * S
    depth = w_in.shape[0]
    qrows = 256
    conv_tile = 256
    row = lambda v: v.astype(F32)[None]
    bias_tab = _bias_table(rel_bias, qrows)
    xf = x.reshape(T, D).astype(F32)
    dense_tiles = jnp.zeros((T // min(1024, T),), jnp.int32)
    cast = _cast_bf16
    w_in_b, wa_b, wb_b, wo_b = cast(w_in), cast(w_branch_attn), cast(w_branch_conv), cast(w_out)
    dense_b = [cast(w)[:, None] for w in (dense_w_gate, dense_w_up, dense_w_down)]
    moe_b = [cast(w) for w in (moe_w_gate, moe_w_up, moe_w_down)]
    for l in range(depth):
        p = _in_proj(xf, row(mix_norm_g[l]), w_in_b, l)
        a = _attention(p, bias_tab, q_norm_g[l], k_norm_g[l], B, S, qrows)
        c = _conv_branch(p, conv_dw_w[l], conv_dw_b[l], conv_ln_g[l], conv_ln_b[l], B, S, conv_tile)
        xf = _merge(p, a, c, xf, wa_b, wb_b, wo_b, l)
        i = l // 2
        if l % 2 == 0:
            xf = _ffn(xf, row(ffn_norm_g[l]), *dense_b, i, dense_tiles, dense_tiles + 1,
                      residual=True, tm=min(1024, T), tf=512)
        else:
            xf = _moe(xf, row(ffn_norm_g[l]), moe_router[i], *moe_b, i)
    return xf.reshape(B, S, D).astype(x.dtype)
```

```python
import functools

import jax
import jax.numpy as jnp
from jax import lax
from jax.experimental import pallas as pl
from jax.experimental.pallas import tpu as pltpu

F32 = jnp.float32
BF16 = jnp.bfloat16

D_MODEL = 2048
CHUNK = 64
LEFT = 8 * CHUNK
N_HEADS = 16
HEAD_DIM = 64
D_ATTN = N_HEADS * HEAD_DIM
MAX_REL = 256
D_CONV = 1024
CONV_WIDTH = 31
D_FF = 7168
N_EXPERTS = 8
EPS = 1e-6
D_IN = 9216
COL_BLOCK = 1024

LANES = 128
V7X_VMEM_BYTES = 64 * 1024 * 1024
VMEM_LIMIT = V7X_VMEM_BYTES - 8 * 1024 * 1024

NEG_INF = -1e30
LOG2_E = 1.4426950408889634


def _params(*sem):
    return pltpu.CompilerParams(dimension_semantics=sem, vmem_limit_bytes=VMEM_LIMIT)


def _rms_rows_to(dst_ref, x_ref, g_ref, rows=64):
    def body(r, carry):
        sl = pl.ds(pl.multiple_of(r * rows, rows), rows)
        x = x_ref[sl, :]
        ms = jnp.mean(x * x, axis=-1, keepdims=True)
        dst_ref[sl, :] = (x * lax.rsqrt(ms + EPS) * g_ref[...]).astype(dst_ref.dtype)
        return carry

    lax.fori_loop(0, x_ref.shape[0] // rows, body, 0, unroll=4)


CAST_BLOCK_BYTES = 8 * 1024 * 1024


def _cast_kernel(x_ref, o_ref):
    o_ref[...] = x_ref[...].astype(o_ref.dtype)


def _cast_bf16(w):
    C = w.shape[-1]
    flat = w.reshape(-1, C)
    rows = min(flat.shape[0], pl.next_power_of_2(CAST_BLOCK_BYTES // (C * 4) + 1) // 2)
    assert flat.shape[0] % rows == 0 and rows % 16 == 0
    out = pl.pallas_call(
        _cast_kernel,
        out_shape=jax.ShapeDtypeStruct(flat.shape, BF16),
        grid=(flat.shape[0] // rows,),
        in_specs=[pl.BlockSpec((rows, C), lambda i: (i, 0))],
        out_specs=pl.BlockSpec((rows, C), lambda i: (i, 0)),
        compiler_params=_params("parallel"),
        name="cast_bf16",
    )(flat)
    return out.reshape(w.shape)


def _in_proj_kernel(x_ref, g_ref, w_ref, o_ref, h_ref):
    @pl.when(pl.program_id(1) == 0)
    def _():
        _rms_rows_to(h_ref, x_ref, g_ref)

    o_ref[...] = jnp.dot(h_ref[...], w_ref[...], preferred_element_type=F32).astype(o_ref.dtype)


def _in_proj(x, g, w, layer):
    T = x.shape[0]
    tm = min(1024, T)
    tn = D_IN // 4
    return pl.pallas_call(
        _in_proj_kernel,
        out_shape=jax.ShapeDtypeStruct((T, D_IN), BF16),
        grid=(T // tm, D_IN // tn),
        in_specs=[
            pl.BlockSpec((tm, D_MODEL), lambda i, j: (i, 0)),
            pl.BlockSpec((1, D_MODEL), lambda i, j: (0, 0)),
            pl.BlockSpec((None, D_MODEL, tn), lambda i, j: (layer, 0, j)),
        ],
        out_specs=pl.BlockSpec((tm, tn), lambda i, j: (i, j)),
        scratch_shapes=[pltpu.VMEM((tm, D_MODEL), BF16)],
        compiler_params=_params("parallel", "arbitrary"),
        name="in_proj",
    )(x, g, w)


def _pair_rms_scale(xp, ones_ref):
    ss = jnp.dot((xp * xp).astype(BF16), ones_ref[...], preferred_element_type=F32)
    return lax.rsqrt(ss * (1.0 / HEAD_DIM) + EPS)


def _attn_kernel(q_ref, k_ref, v_ref, bias_ref, gq_ref, gk_ref, ones_ref, o_ref, kn_ref, qn_ref, *, qrows, win):
    qb = pl.program_id(1)
    n_pairs = D_ATTN // LANES

    @pl.when(qb == 0)
    def _():
        rows = 128

        def body(r, carry):
            sl = pl.ds(pl.multiple_of(r * rows, rows), rows)
            for hp in range(n_pairs):
                cs = slice(hp * LANES, (hp + 1) * LANES)
                kp = k_ref[sl, cs].astype(F32)
                kn_ref[sl, cs] = (kp * _pair_rms_scale(kp, ones_ref) * gk_ref[...]).astype(BF16)
            return carry

        lax.fori_loop(0, k_ref.shape[0] // rows, body, 0)

    lane = lax.broadcasted_iota(jnp.int32, (qrows, LANES), 1)
    lo = lane < HEAD_DIM
    for hp in range(n_pairs):
        cs = slice(hp * LANES, (hp + 1) * LANES)
        qp = q_ref[:, cs].astype(F32)
        qn = qp * _pair_rms_scale(qp, ones_ref) * gq_ref[...]
        qn_ref[hp, 0:qrows, :] = jnp.where(lo, qn, 0.0).astype(BF16)
        qn_ref[hp, qrows:2 * qrows, :] = jnp.where(lo, 0.0, qn).astype(BF16)

    ws = pl.multiple_of(jnp.maximum(qb * qrows - LEFT, 0), CHUNK)
    for hp in range(n_pairs):
        cs = slice(hp * LANES, (hp + 1) * LANES)
        kw = kn_ref[pl.ds(ws, win), cs]
        vw = v_ref[pl.ds(ws, win), cs]
        s = lax.dot_general(qn_ref[hp], kw, (((1,), (1,)), ((), ())), preferred_element_type=F32)
        s = s + bias_ref[2 * hp:2 * hp + 2].reshape(2 * qrows, win)
        m = jnp.max(s, axis=-1, keepdims=True)
        e = jnp.exp2(s - m)
        vx = jnp.concatenate([vw, jnp.ones_like(vw)], axis=1)
        ox = jnp.dot(e.astype(BF16), vx, preferred_element_type=F32)
        o = ox[:, 0:LANES] / ox[:, LANES:2 * LANES]
        o_ref[:, cs] = jnp.where(lo, o[0:qrows], o[qrows:2 * qrows]).astype(o_ref.dtype)


def _bias_table(rel_bias, qrows):
    win = LEFT + qrows
    n_var = LEFT // qrows + 1
    period = win + qrows
    diag = jnp.concatenate([jnp.arange(win), jnp.arange(-qrows, 0)])
    off = (jnp.arange(n_var) * qrows)[:, None]
    rel = jnp.clip(off - diag[None, :], -MAX_REL, MAX_REL) + MAX_REL
    vec = jnp.transpose(rel_bias.astype(F32)[:, rel], (1, 0, 2))
    flat = jnp.tile(vec, (1, 1, qrows))[..., :qrows * (period - 1)]
    bias = flat.reshape(n_var, N_HEADS, qrows, period - 1)[..., :win]
    qi = off[:, :, None] + jnp.arange(qrows)[None, :, None]
    kj = jnp.arange(win)[None, None, :]
    dchunk = qi // CHUNK - kj // CHUNK
    valid = (dchunk >= 0) & (dchunk <= LEFT // CHUNK)
    return jnp.where(valid[:, None], bias * LOG2_E, NEG_INF)


def _attention(p, bias_tab, gq, gk, B, S, qrows):
    T = B * S
    win = LEFT + qrows
    nq = S // qrows
    n_var = bias_tab.shape[0]
    scale = HEAD_DIM ** -0.5
    gq2 = (jnp.tile(gq.astype(F32), 2) * (scale * LOG2_E))[None]
    gk2 = jnp.tile(gk.astype(F32), 2)[None]
    head_id = jnp.arange(LANES) // HEAD_DIM
    head_ones = (head_id[:, None] == head_id[None, :]).astype(BF16)
    kern = functools.partial(_attn_kernel, qrows=qrows, win=win)
    return pl.pallas_call(
        kern,
        out_shape=jax.ShapeDtypeStruct((T, D_ATTN), BF16),
        grid=(B, nq),
        in_specs=[
            pl.BlockSpec((qrows, COL_BLOCK), lambda b, q: (b * nq + q, 0)),
            pl.BlockSpec((S, COL_BLOCK), lambda b, q: (b, 1), pipeline_mode=pl.Buffered(1)),
            pl.BlockSpec((S, COL_BLOCK), lambda b, q: (b, 2)),
            pl.BlockSpec((None, N_HEADS, qrows, win), lambda b, q: (jnp.minimum(q, n_var - 1), 0, 0, 0)),
            pl.BlockSpec((1, LANES), lambda b, q: (0, 0)),
            pl.BlockSpec((1, LANES), lambda b, q: (0, 0)),
            pl.BlockSpec((LANES, LANES), lambda b, q: (0, 0)),
        ],
        out_specs=pl.BlockSpec((qrows, D_ATTN), lambda b, q: (b * nq + q, 0)),
        scratch_shapes=[pltpu.VMEM((S, D_ATTN), BF16), pltpu.VMEM((D_ATTN // LANES, 2 * qrows, LANES), BF16)],
        compiler_params=_params("parallel", "arbitrary"),
        name="band_attention",
    )(p, p, p, bias_tab, gq2, gk2, head_ones)


CONV_HIST = 32
SUBLANES = 8
CONV_ROWS = 2 * SUBLANES
CONV_SHIFT_ROWS = CONV_HIST - SUBLANES


def _conv_kernel(a_ref, g_ref, w_ref, b_ref, lg_ref, lb_ref, o_ref, y_ref, ys_ref, *, tt):
    t = pl.program_id(1)

    @pl.when(t == 0)
    def _():
        y_ref[0:CONV_HIST, :] = jnp.zeros((CONV_HIST, D_CONV), F32)

    @pl.when(t > 0)
    def _():
        y_ref[0:CONV_HIST, :] = y_ref[tt:tt + CONV_HIST, :]

    a = a_ref[...].astype(F32)
    g = g_ref[...].astype(F32)
    y_ref[CONV_HIST:CONV_HIST + tt, :] = a * jax.nn.sigmoid(g)

    n_shift = tt + CONV_SHIFT_ROWS
    for r in range(1, SUBLANES):
        ys_ref[r - 1] = y_ref[r:r + n_shift, :]

    shift = CONV_HIST - (CONV_WIDTH - 1)
    halves = CONV_ROWS // SUBLANES
    for rb in range(tt // CONV_ROWS):
        r0 = rb * CONV_ROWS
        acc = [b_ref[...] for _ in range(halves)]
        for j in range(CONV_WIDTH):
            a8, r = divmod(shift + j, SUBLANES)
            src = y_ref if r == 0 else ys_ref.at[r - 1]
            wj = w_ref[j]
            for hf in range(halves):
                m0 = r0 + (a8 + hf) * SUBLANES
                acc[hf] = acc[hf] + wj * src[m0:m0 + SUBLANES, :]
        for hf in range(halves):
            mu = jnp.mean(acc[hf], axis=-1, keepdims=True)
            xc = acc[hf] - mu
            var = jnp.mean(xc * xc, axis=-1, keepdims=True)
            z = xc * lax.rsqrt(var + EPS) * lg_ref[...] + lb_ref[...]
            rows = slice(r0 + hf * SUBLANES, r0 + (hf + 1) * SUBLANES)
            o_ref[rows, :] = (z * jax.nn.sigmoid(z)).astype(o_ref.dtype)


def _conv_branch(p, w_dw, b_dw, ln_g, ln_b, B, S, tt):
    T = B * S
    nt = S // tt
    rep = lambda v: jnp.broadcast_to(v.astype(F32)[..., None, :], v.shape[:-1] + (SUBLANES, D_CONV))
    vec = pl.BlockSpec((SUBLANES, D_CONV), lambda b, t: (0, 0))
    return pl.pallas_call(
        functools.partial(_conv_kernel, tt=tt),
        out_shape=jax.ShapeDtypeStruct((T, D_CONV), BF16),
        grid=(B, nt),
        in_specs=[
            pl.BlockSpec((tt, COL_BLOCK), lambda b, t: (b * nt + t, 3)),
            pl.BlockSpec((tt, COL_BLOCK), lambda b, t: (b * nt + t, 4)),
            pl.BlockSpec((CONV_WIDTH, SUBLANES, D_CONV), lambda b, t: (0, 0, 0)),
            vec, vec, vec,
        ],
        out_specs=pl.BlockSpec((tt, D_CONV), lambda b, t: (b * nt + t, 0)),
        scratch_shapes=[pltpu.VMEM((CONV_HIST + tt, D_CONV), F32),
                        pltpu.VMEM((SUBLANES - 1, tt + CONV_SHIFT_ROWS, D_CONV), F32)],
        compiler_params=_params("parallel", "arbitrary"),
        name="conv_branch",
    )(p, p, rep(w_dw), rep(b_dw), rep(ln_g), rep(ln_b))


MERGE_COLS = 512


def _merge_kernel(a_ref, c_ref, ga0_ref, ga1_ref, gc0_ref, gc1_ref, x_ref, wa_ref, wb_ref, wo_ref, o_ref, m_ref):
    ga_refs = (ga0_ref, ga1_ref)
    gc_refs = (gc0_ref, gc1_ref)
    per_block = COL_BLOCK // MERGE_COLS
    for n in range(D_MODEL // MERGE_COLS):
        cs = slice(n * MERGE_COLS, (n + 1) * MERGE_COLS)
        gs = slice((n % per_block) * MERGE_COLS, (n % per_block + 1) * MERGE_COLS)
        ya = jnp.dot(a_ref[...], wa_ref[:, cs], preferred_element_type=F32)
        yc = jnp.dot(c_ref[...], wb_ref[:, cs], preferred_element_type=F32)
        ga = jax.nn.sigmoid(ga_refs[n // per_block][:, gs].astype(F32))
        gc = jax.nn.sigmoid(gc_refs[n // per_block][:, gs].astype(F32))
        m_ref[:, cs] = (ga * ya + gc * yc).astype(BF16)
    for n in range(D_MODEL // MERGE_COLS):
        cs = slice(n * MERGE_COLS, (n + 1) * MERGE_COLS)
        o_ref[:, cs] = x_ref[:, cs] + jnp.dot(m_ref[...], wo_ref[:, cs], preferred_element_type=F32)


def _merge(p, a, c, x, wa, wb, wo, layer):
    T = x.shape[0]
    tm = min(512, T)
    gate = lambda blk: pl.BlockSpec((tm, COL_BLOCK), lambda i: (i, blk))
    resident = lambda shape: pl.BlockSpec((None,) + shape, lambda i: (layer, 0, 0), pipeline_mode=pl.Buffered(1))
    return pl.pallas_call(
        _merge_kernel,
        out_shape=jax.ShapeDtypeStruct((T, D_MODEL), F32),
        grid=(T // tm,),
        in_specs=[
            pl.BlockSpec((tm, D_ATTN), lambda i: (i, 0)),
            pl.BlockSpec((tm, D_CONV), lambda i: (i, 0)),
            gate(5), gate(6), gate(7), gate(8),
            pl.BlockSpec((tm, D_MODEL), lambda i: (i, 0)),
            resident((D_ATTN, D_MODEL)),
            resident((D_CONV, D_MODEL)),
            resident((D_MODEL, D_MODEL)),
        ],
        out_specs=pl.BlockSpec((tm, D_MODEL), lambda i: (i, 0)),
        scratch_shapes=[pltpu.VMEM((tm, D_MODEL), BF16)],
        compiler_params=_params("parallel"),
        name="merge_out_proj",
    )(a, c, p, p, p, p, x, wa, wb, wo)


def _ffn_kernel(te_ref, ta_ref, x_ref, g_ref, wg_ref, wu_ref, wd_ref, o_ref, h_ref, *, residual):
    del te_ref
    i = pl.program_id(0)
    f = pl.program_id(1)

    @pl.when(f == 0)
    def _():
        _rms_rows_to(h_ref, x_ref, g_ref)
        if residual:
            o_ref[...] = x_ref[...]
        else:
            o_ref[...] = jnp.zeros(o_ref.shape, o_ref.dtype)

    @pl.when(ta_ref[i] == 1)
    def _():
        h = h_ref[...]
        gt = jnp.dot(h, wg_ref[...], preferred_element_type=F32)
        up = jnp.dot(h, wu_ref[...], preferred_element_type=F32)
        act = (gt * jax.nn.sigmoid(gt) * up).astype(BF16)
        o_ref[...] += jnp.dot(act, wd_ref[...], preferred_element_type=F32)


def _ffn(x, g, wg, wu, wd, layer, tile_expert, tile_active, *, residual, tm, tf):
    R = x.shape[0]
    nf = D_FF // tf
    fsel = lambda i, f, te, ta: jnp.where(ta[i] == 1, f, nf - 1)
    up_spec = pl.BlockSpec((None, None, D_MODEL, tf), lambda i, f, te, ta: (layer, te[i], 0, fsel(i, f, te, ta)))
    down_spec = pl.BlockSpec((None, None, tf, D_MODEL), lambda i, f, te, ta: (layer, te[i], fsel(i, f, te, ta), 0))
    grid_spec = pltpu.PrefetchScalarGridSpec(
        num_scalar_prefetch=2,
        grid=(R // tm, nf),
        in_specs=[
            pl.BlockSpec((tm, D_MODEL), lambda i, f, te, ta: (i, 0)),
            pl.BlockSpec((1, D_MODEL), lambda i, f, te, ta: (0, 0)),
            up_spec, up_spec, down_spec,
        ],
        out_specs=pl.BlockSpec((tm, D_MODEL), lambda i, f, te, ta: (i, 0)),
        scratch_shapes=[pltpu.VMEM((tm, D_MODEL), BF16)],
    )
    return pl.pallas_call(
        functools.partial(_ffn_kernel, residual=residual),
        out_shape=jax.ShapeDtypeStruct((R, D_MODEL), F32),
        grid_spec=grid_spec,
        compiler_params=_params("parallel", "arbitrary"),
        name="swiglu_residual" if residual else "swiglu_grouped",
    )(tile_expert, tile_active, x, g, wg, wu, wd)


ROUTER_ROWS = 128


def _router_kernel(x_ref, g_ref, wh_ref, wl_ref, meta_ref, gate_ref, cnt_ref, tri_ref, carry_ref, logit_ref):
    i = pl.program_id(0)
    tm = x_ref.shape[0]

    @pl.when(i == 0)
    def _():
        r = lax.broadcasted_iota(jnp.int32, (tm, tm), 0)
        c = lax.broadcasted_iota(jnp.int32, (tm, tm), 1)
        tri_ref[...] = jnp.where(c < r, 1.0, 0.0).astype(BF16)
        carry_ref[...] = jnp.zeros(carry_ref.shape, F32)

    def body(r, carry):
        sl = pl.ds(pl.multiple_of(r * ROUTER_ROWS, ROUTER_ROWS), ROUTER_ROWS)
        x = x_ref[sl, :]
        ms = jnp.mean(x * x, axis=-1, keepdims=True)
        h = x * lax.rsqrt(ms + EPS) * g_ref[...]
        hi = h.astype(BF16)
        lo = (h - hi.astype(F32)).astype(BF16)
        logit_ref[sl, :] = (jnp.dot(hi, wh_ref[...], preferred_element_type=F32)
                            + jnp.dot(lo, wh_ref[...], preferred_element_type=F32)
                            + jnp.dot(hi, wl_ref[...], preferred_element_type=F32))
        return carry

    lax.fori_loop(0, tm // ROUTER_ROWS, body, 0)

    lane = lax.broadcasted_iota(jnp.int32, (tm, LANES), 1).astype(F32)
    logits = jnp.where(lane < N_EXPERTS, logit_ref[...], -jnp.inf)
    m0 = jnp.max(logits, axis=-1, keepdims=True)
    e0 = jnp.min(jnp.where(logits == m0, lane, float(LANES)), axis=-1, keepdims=True)
    rest = jnp.where(lane == e0, -jnp.inf, logits)
    m1 = jnp.max(rest, axis=-1, keepdims=True)
    e1 = jnp.min(jnp.where(rest == m1, lane, float(LANES)), axis=-1, keepdims=True)
    t = jnp.exp(m1 - m0)
    w0 = 1.0 / (1.0 + t)
    w1 = t / (1.0 + t)

    onehot = jnp.where(jnp.logical_or(lane == e0, lane == e1), 1.0, 0.0)
    before = jnp.dot(tri_ref[...], onehot.astype(BF16), preferred_element_type=F32) + carry_ref[0:1, :]
    r0 = jnp.sum(jnp.where(lane == e0, before, 0.0), axis=-1, keepdims=True)
    r1 = jnp.sum(jnp.where(lane == e1, before, 0.0), axis=-1, keepdims=True)
    carry_ref[0:1, :] = carry_ref[0:1, :] + jnp.sum(onehot, axis=0, keepdims=True)

    meta = jnp.where(lane == 0, e0, jnp.where(lane == 1, e1, jnp.where(lane == 2, r0, jnp.where(lane == 3, r1, 0.0))))
    meta_ref[...] = meta.astype(jnp.int32)
    gate_ref[...] = jnp.where(lane == 0, w0, jnp.where(lane == 1, w1, 0.0))
    cnt_ref[...] = carry_ref[...]


def _router(x, g, w_router):
    T = x.shape[0]
    tm = min(1024, T)
    w_pad = jnp.zeros((D_MODEL, LANES), F32).at[:, :N_EXPERTS].set(w_router.astype(F32))
    wh = w_pad.astype(BF16)
    wl = (w_pad - wh.astype(F32)).astype(BF16)
    wspec = pl.BlockSpec((D_MODEL, LANES), lambda i: (0, 0))
    return pl.pallas_call(
        _router_kernel,
        out_shape=(jax.ShapeDtypeStruct((T, LANES), jnp.int32),
                   jax.ShapeDtypeStruct((T, LANES), F32),
                   jax.ShapeDtypeStruct((8, LANES), F32)),
        grid=(T // tm,),
        in_specs=[pl.BlockSpec((tm, D_MODEL), lambda i: (i, 0)), pl.BlockSpec((1, D_MODEL), lambda i: (0, 0)), wspec, wspec],
        out_specs=(pl.BlockSpec((tm, LANES), lambda i: (i, 0)),
                   pl.BlockSpec((tm, LANES), lambda i: (i, 0)),
                   pl.BlockSpec((8, LANES), lambda i: (0, 0))),
        scratch_shapes=[pltpu.VMEM((tm, tm), BF16), pltpu.VMEM((8, LANES), F32), pltpu.VMEM((tm, LANES), F32)],
        compiler_params=_params("arbitrary"),
        name="moe_router",
    )(x, g, wh, wl)


ROW_DMA_UNROLL = 8


def _scatter_rows_kernel(dest_ref, pad_ref, x_ref, o_hbm, zero_ref, sem, *, tc, pc):
    zero_ref[...] = jnp.zeros(zero_ref.shape, zero_ref.dtype)

    def copy(t, k):
        d = dest_ref[0, 0, 2 * t + k]
        return pltpu.make_async_copy(x_ref.at[pl.ds(t, 1)], o_hbm.at[pl.ds(d, 1)], sem)

    def fill(j):
        d = pad_ref[0, 0, j]
        return pltpu.make_async_copy(zero_ref.at[pl.ds(0, 1)], o_hbm.at[pl.ds(d, 1)], sem)

    def issue(t, carry):
        copy(t, 0).start()
        copy(t, 1).start()
        return carry

    def drain(t, carry):
        copy(t, 0).wait()
        copy(t, 1).wait()
        return carry

    def issue_fill(j, carry):
        fill(j).start()
        return carry

    def drain_fill(j, carry):
        fill(j).wait()
        return carry

    lax.fori_loop(0, tc, issue, 0, unroll=ROW_DMA_UNROLL)
    lax.fori_loop(0, pc, issue_fill, 0, unroll=ROW_DMA_UNROLL)
    lax.fori_loop(0, tc, drain, 0, unroll=ROW_DMA_UNROLL)
    lax.fori_loop(0, pc, drain_fill, 0, unroll=ROW_DMA_UNROLL)


def _scatter_rows(x, dest, pad_rows, n_rows):
    T = x.shape[0]
    tc = min(256, T)
    n = T // tc
    assert n_rows == 2 * T + pad_rows.shape[0] and pad_rows.shape[0] % n == 0
    pc = pad_rows.shape[0] // n
    return pl.pallas_call(
        functools.partial(_scatter_rows_kernel, tc=tc, pc=pc),
        out_shape=jax.ShapeDtypeStruct((n_rows, D_MODEL), x.dtype),
        grid=(n,),
        in_specs=[
            pl.BlockSpec((1, 1, 2 * tc), lambda i: (i, 0, 0), memory_space=pltpu.SMEM),
            pl.BlockSpec((1, 1, pc), lambda i: (i, 0, 0), memory_space=pltpu.SMEM),
            pl.BlockSpec((tc, D_MODEL), lambda i: (i, 0)),
        ],
        out_specs=pl.BlockSpec(memory_space=pl.ANY),
        scratch_shapes=[pltpu.VMEM((SUBLANES, D_MODEL), x.dtype), pltpu.SemaphoreType.DMA(())],
        compiler_params=_params("arbitrary"),
        name="moe_scatter_rows",
    )(dest.reshape(n, 1, 2 * tc), pad_rows.reshape(n, 1, pc), x)


def _combine_kernel(dest_ref, next_ref, x_ref, gate_ref, y_hbm, o_ref, buf_ref, sem, *, tc):
    i = pl.program_id(0)
    slot = lax.rem(i, 2)

    def copy(idx_ref, s, t, k):
        d = idx_ref[0, 0, 2 * t + k]
        return pltpu.make_async_copy(y_hbm.at[pl.ds(d, 1)], buf_ref.at[s, k, pl.ds(t, 1)], sem.at[s])

    def start_tile(idx_ref, s):
        def body(t, carry):
            copy(idx_ref, s, t, 0).start()
            copy(idx_ref, s, t, 1).start()
            return carry

        lax.fori_loop(0, tc, body, 0, unroll=ROW_DMA_UNROLL)

    @pl.when(i == 0)
    def _():
        start_tile(dest_ref, slot)

    @pl.when(i + 1 < pl.num_programs(0))
    def _():
        start_tile(next_ref, 1 - slot)

    def drain(t, carry):
        copy(dest_ref, slot, t, 0).wait()
        copy(dest_ref, slot, t, 1).wait()
        return carry

    lax.fori_loop(0, tc, drain, 0, unroll=ROW_DMA_UNROLL)
    w0 = gate_ref[:, 0:1]
    w1 = gate_ref[:, 1:2]
    o_ref[...] = x_ref[...] + w0 * buf_ref[slot, 0] + w1 * buf_ref[slot, 1]


def _combine(x, gate, y, dest):
    T = x.shape[0]
    tc = min(256, T)
    n = T // tc
    dest3 = dest.reshape(n, 1, 2 * tc)
    return pl.pallas_call(
        functools.partial(_combine_kernel, tc=tc),
        out_shape=jax.ShapeDtypeStruct((T, D_MODEL), F32),
        grid=(n,),
        in_specs=[
            pl.BlockSpec((1, 1, 2 * tc), lambda i: (i, 0, 0), memory_space=pltpu.SMEM),
            pl.BlockSpec((1, 1, 2 * tc), lambda i: (jnp.minimum(i + 1, n - 1), 0, 0), memory_space=pltpu.SMEM),
            pl.BlockSpec((tc, D_MODEL), lambda i: (i, 0)),
            pl.BlockSpec((tc, LANES), lambda i: (i, 0)),
            pl.BlockSpec(memory_space=pl.ANY),
        ],
        out_specs=pl.BlockSpec((tc, D_MODEL), lambda i: (i, 0)),
        scratch_shapes=[pltpu.VMEM((2, 2, tc, D_MODEL), F32), pltpu.SemaphoreType.DMA((2,))],
        compiler_params=_params("arbitrary"),
        name="moe_combine",
    )(dest3, dest3, x, gate, y)


def _moe(x, g, w_router, wg, wu, wd, layer):
    T = x.shape[0]
    tm = min(512, T)
    meta, gate, cnt = _router(x, g, w_router)
    expert = meta[:, 0:2]
    rank = meta[:, 2:4]
    counts = cnt[0, :N_EXPERTS].astype(jnp.int32)
    tiles_per = (counts + tm - 1) // tm
    tile_end = jnp.cumsum(tiles_per)
    offset = (tile_end - tiles_per) * tm
    dest = offset[expert] + rank
    n_tiles = (2 * T) // tm + N_EXPERTS
    tile_id = jnp.arange(n_tiles)
    tile_active = (tile_id < tile_end[-1]).astype(jnp.int32)
    last_tile = jnp.minimum(tile_id, tile_end[-1] - 1)
    tile_expert = jnp.sum(last_tile[:, None] >= tile_end[None, :], axis=1).astype(jnp.int32)
    n_rows = n_tiles * tm
    seg_start = jnp.concatenate([offset + counts, tile_end[-1:] * tm])
    seg_len = jnp.concatenate([tiles_per * tm - counts, n_rows - tile_end[-1:] * tm])
    seg_end = jnp.cumsum(seg_len)
    j = jnp.arange(n_rows - 2 * T)
    seg = jnp.sum(j[:, None] >= seg_end[None, :], axis=1)
    pad_rows = (seg_start[seg] + j - (seg_end - seg_len)[seg]).astype(jnp.int32)
    xs = _scatter_rows(x, dest, pad_rows, n_rows)
    ys = _ffn(xs, g, wg, wu, wd, layer, tile_expert, tile_active, residual=False, tm=tm, tf=1024)
    return _combine(x, gate, ys, dest)


def kernel(x, mix_norm_g, w_in, q_norm_g, k_norm_g, rel_bias, conv_dw_w, conv_dw_b, conv_ln_g, conv_ln_b,
           w_branch_attn, w_branch_conv, w_out, ffn_norm_g, dense_w_gate, dense_w_up, dense_w_down,
           moe_router, moe_w_gate, moe_w_up, moe_w_down):
    B, S, D = x.shape
    T = B * S
    depth = w_in.shape[0]
    qrows = 256
    conv_tile = 512
    row = lambda v: v.astype(F32)[None]
    bias_tab = _bias_table(rel_bias, qrows)
    xf = x.reshape(T, D).astype(F32)
    dense_tiles = jnp.zeros((T // min(1024, T),), jnp.int32)
    cast = _cast_bf16
    w_in_b, wa_b, wb_b, wo_b = cast(w_in), cast(w_branch_attn), cast(w_branch_conv), cast(w_out)
    dense_b = [cast(w)[:, None] for w in (dense_w_gate, dense_w_up, dense_w_down)]
    moe_b = [cast(w) for w in (moe_w_gate, moe_w_up, moe_w_down)]
    for l in range(depth):
        p = _in_proj(xf, row(mix_norm_g[l]), w_in_b, l)
        a = _attention(p, bias_tab, q_norm_g[l], k_norm_g[l], B, S, qrows)
        c = _conv_branch(p, conv_dw_w[l], conv_dw_b[l], conv_ln_g[l], conv_ln_b[l], B, S, conv_tile)
        xf = _merge(p, a, c, xf, wa_b, wb_b, wo_b, l)
        i = l // 2
        if l % 2 == 0:
            xf = _ffn(xf, row(ffn_norm_g[l]), *dense_b, i, dense_tiles, dense_tiles + 1,
                      residual=True, tm=min(1024, T), tf=512)
        else:
            xf = _moe(xf, row(ffn_norm_g[l]), moe_router[i], *moe_b, i)
    return xf.reshape(B, S, D).astype(x.dtype)
```
